```python
import math
import jax
import jax.numpy as jnp
from jax import lax
import numpy as np

D_MODEL = 2048
BATCH = 16
SEQ = 2048
DEPTH = 4

CTX_LEN = 256
GRID_W = 64
N_HEADS = 16
N_KV_HEADS = 4
HEAD_DIM = 128
ATT_WIDTH = N_HEADS * HEAD_DIM
KV_WIDTH = N_KV_HEADS * HEAD_DIM
WINDOW = 128
ATT_BLOCK = 128
ROPE_THETA = 10000.0
SSD_EXPAND = 2
D_INNER = SSD_EXPAND * D_MODEL
SSD_HEAD_DIM = 64
SSD_HEADS = D_INNER // SSD_HEAD_DIM
SSD_GROUPS = 8
HEADS_PER_GROUP = SSD_HEADS // SSD_GROUPS
D_STATE = 128
CONV_K = 5
SSD_CHUNK = 128
CONV_CH = D_INNER + 2 * SSD_GROUPS * D_STATE
D_FF = 4 * D_MODEL
N_BRANCHES = 2
N_MOD = 6
EPS = 1e-6

COL_K = 0
COL_V = COL_K + KV_WIDTH
COL_XBC = COL_V + KV_WIDTH
COL_DT = COL_XBC + CONV_CH
COL_Q = COL_DT + 2 * SSD_HEADS
CTX_SIDE_COLS = COL_Q
COL_Z = COL_Q + ATT_WIDTH
COL_GATE = COL_Z + D_INNER
IN_COLS = COL_GATE + N_BRANCHES * D_MODEL

F32 = jnp.float32

kernel_name = 'hybrid_swa_ssd_dit_block'


def rms_norm(x, g):
    xf = x.astype(F32)
    y = xf * lax.rsqrt(jnp.mean(xf * xf, axis=-1, keepdims=True) + EPS)
    return (y * g.astype(F32)).astype(x.dtype)


def modulate(h, shift, scale):
    return h * (1 + scale) + shift


def squared_relu_mlp(h, w1, w2):
    return jnp.square(jax.nn.relu(h @ w1)) @ w2


def axial_rope_tables(rows):
    t = jnp.arange(rows * GRID_W)
    row = (t // GRID_W).astype(F32)
    col = (t % GRID_W).astype(F32)
    axis_dim = HEAD_DIM // 2
    inv_freq = ROPE_THETA ** (-jnp.arange(0, axis_dim, 2, dtype=F32) / axis_dim)
    ang_r = row[:, None] * inv_freq[None]
    ang_c = col[:, None] * inv_freq[None]
    return (jnp.cos(ang_r), jnp.sin(ang_r), jnp.cos(ang_c), jnp.sin(ang_c))


def _rotate(u, cos, sin):
    u1, u2 = jnp.split(u, 2, axis=-1)
    cos = cos[None, :, None, :]
    sin = sin[None, :, None, :]
    return jnp.concatenate([u1 * cos - u2 * sin, u2 * cos + u1 * sin], axis=-1)


def apply_axial_rope(u, tables):
    cos_r, sin_r, cos_c, sin_c = tables
    uf = u.astype(F32)
    half = HEAD_DIM // 2
    out = jnp.concatenate([_rotate(uf[..., :half], cos_r, sin_r),
                           _rotate(uf[..., half:], cos_c, sin_c)], axis=-1)
    return out.astype(u.dtype)


def window_attention_latent(q, k, v, k_ctx, v_ctx, sink):
    b, s = q.shape[:2]
    l = k_ctx.shape[1]
    nb = s // ATT_BLOCK
    nbr = -(-WINDOW // ATT_BLOCK)
    span = (2 * nbr + 1) * ATT_BLOCK
    rep = N_HEADS // N_KV_HEADS
    scale = HEAD_DIM ** -0.5
    qb = q.reshape(b, nb, ATT_BLOCK, N_KV_HEADS, rep, HEAD_DIM)

    def band(t):
        tp = jnp.pad(t, ((0, 0), (nbr * ATT_BLOCK, nbr * ATT_BLOCK), (0, 0), (0, 0)))
        tb = tp.reshape(b, nb + 2 * nbr, ATT_BLOCK, N_KV_HEADS, HEAD_DIM)
        return jnp.concatenate([tb[:, o:o + nb] for o in range(2 * nbr + 1)], axis=2)

    kw, vw = band(k), band(v)
    qpos = jnp.arange(s).reshape(nb, ATT_BLOCK)
    kpos = (jnp.arange(nb)[:, None] - nbr) * ATT_BLOCK + jnp.arange(span)[None]
    in_range = ((kpos >= 0) & (kpos < s))[:, None, :]
    valid = (jnp.abs(kpos[:, None, :] - qpos[:, :, None]) <= WINDOW) & in_range

    s_win = jnp.einsum('bnqhrd,bnkhd->bnhrqk', qb, kw).astype(F32) * scale
    s_win = jnp.where(valid[None, :, None, None], s_win, -jnp.inf)
    s_ctx = jnp.einsum('bnqhrd,blhd->bnhrql', qb, k_ctx).astype(F32) * scale
    s_sink = jnp.broadcast_to(sink.astype(F32).reshape(N_KV_HEADS, rep)[None, None, :, :, None, None],
                              s_win.shape[:-1] + (1,))
    p = jax.nn.softmax(jnp.concatenate([s_win, s_ctx, s_sink], axis=-1), axis=-1)
    p_win = p[..., :span].astype(v.dtype)
    p_ctx = p[..., span:span + l].astype(v.dtype)
    o = (jnp.einsum('bnhrqk,bnkhd->bnqhrd', p_win, vw)
         + jnp.einsum('bnhrql,blhd->bnqhrd', p_ctx, v_ctx))
    return o.reshape(b, s, ATT_WIDTH)


def context_attention(q, k, v, sink):
    b, l = q.shape[:2]
    rep = N_HEADS // N_KV_HEADS
    qg = q.reshape(b, l, N_KV_HEADS, rep, HEAD_DIM)
    sc = jnp.einsum('bqhrd,bkhd->bhrqk', qg, k).astype(F32) * HEAD_DIM ** -0.5
    s_sink = jnp.broadcast_to(sink.astype(F32).reshape(N_KV_HEADS, rep)[None, :, :, None, None],
                              sc.shape[:-1] + (1,))
    p = jax.nn.softmax(jnp.concatenate([sc, s_sink], axis=-1), axis=-1)
    o = jnp.einsum('bhrqk,bkhd->bqhrd', p[..., :l].astype(v.dtype), v)
    return o.reshape(b, l, ATT_WIDTH)


def centred_depthwise_conv(u, w, bias):
    pad = CONV_K // 2
    out = lax.conv_general_dilated(u, w[:, None, :].astype(u.dtype), window_strides=(1,),
                                   padding=[(pad, pad)], dimension_numbers=('NWC', 'WIO', 'NWC'),
                                   feature_group_count=u.shape[-1])
    return out + bias.astype(u.dtype)


def ssd_chunked(xh, dt, a_neg, bm, cm, init_state, with_output):
    b, t = xh.shape[:2]
    nc = t // SSD_CHUNK
    x = xh.astype(F32).reshape(b, nc, SSD_CHUNK, SSD_GROUPS, HEADS_PER_GROUP, SSD_HEAD_DIM)
    dt = dt.reshape(b, nc, SSD_CHUNK, SSD_GROUPS, HEADS_PER_GROUP)
    bm = bm.astype(F32).reshape(b, nc, SSD_CHUNK, SSD_GROUPS, D_STATE)
    cm = cm.astype(F32).reshape(b, nc, SSD_CHUNK, SSD_GROUPS, D_STATE)
    a_cum = jnp.cumsum(dt * a_neg.reshape(SSD_GROUPS, HEADS_PER_GROUP), axis=2)
    a_last = a_cum[:, :, -1]
    xdt = x * dt[..., None]
    w_end = jnp.exp(a_last[:, :, None] - a_cum)[..., None] * xdt
    chunk_states = jnp.einsum('bcqgn,bcqgep->bcgepn', bm, w_end)

    def step(state, inp):
        decay, new = inp
        return decay[..., None, None] * state + new, state

    final, s_in = lax.scan(step, init_state,
                           (jnp.moveaxis(jnp.exp(a_last), 1, 0), jnp.moveaxis(chunk_states, 1, 0)))
    if not with_output:
        return final
    s_in = jnp.moveaxis(s_in, 0, 1)
    seg = a_cum[:, :, :, None] - a_cum[:, :, None, :]
    lower = jnp.tril(jnp.ones((SSD_CHUNK, SSD_CHUNK), dtype=bool))[None, None, :, :, None, None]
    decay = jnp.exp(jnp.where(lower, seg, -jnp.inf))
    cb = jnp.einsum('bcign,bcjgn->bcijg', cm, bm)
    y_diag = jnp.einsum('bcijge,bcjgep->bcigep', cb[..., None] * decay, xdt)
    y_off = jnp.einsum('bcign,bcgepn->bcigep', cm, s_in) * jnp.exp(a_cum)[..., None]
    y = (y_diag + y_off).reshape(b, t, SSD_HEADS, SSD_HEAD_DIM)
    return y, final


def _time_order(u, reverse):
    return jnp.flip(u, axis=1) if reverse else u


def ssd_bidirectional(xs, dt_raw, bm, cm, dt_bias, a_log, d_skip, inits, with_output):
    y_sum = None
    finals = []
    for d in range(2):
        rev = d == 1
        dt = jax.nn.softplus(dt_raw[:, :, d].astype(F32) + dt_bias[d].astype(F32))
        a_neg = -jnp.exp(a_log[d].astype(F32))
        out = ssd_chunked(_time_order(xs, rev), _time_order(dt, rev), a_neg,
                          _time_order(bm, rev), _time_order(cm, rev), inits[d], with_output)
        if with_output:
            y, fin = out
            y = _time_order(y, rev) + xs.astype(F32) * d_skip[d].astype(F32)[:, None]
            y_sum = y if y_sum is None else y_sum + y
        else:
            fin = out
        finals.append(fin)
    return y_sum, finals


def _heads(t, lo, width, n_heads):
    return t[..., lo:lo + width].reshape(t.shape[:2] + (n_heads, width // n_heads))


def _ssd_inputs(p, conv_w, conv_b):
    xbc = jax.nn.silu(centred_depthwise_conv(p[..., COL_XBC:COL_DT], conv_w, conv_b))
    lead = p.shape[:2]
    gn = SSD_GROUPS * D_STATE
    xs = xbc[..., :D_INNER].reshape(lead + (SSD_HEADS, SSD_HEAD_DIM))
    bm = xbc[..., D_INNER:D_INNER + gn].reshape(lead + (SSD_GROUPS, D_STATE))
    cm = xbc[..., D_INNER + gn:].reshape(lead + (SSD_GROUPS, D_STATE))
    dt_raw = p[..., COL_DT:COL_Q].reshape(lead + (2, SSD_HEADS))
    return xs, dt_raw, bm, cm


def hybrid_mixer(h_x, h_c, rope, w_in, sink, conv_w, conv_b, dt_bias, a_log, d_skip, g_ssd,
                 w_o_attn, w_o_ssd, w_out, ctx_out):
    b = h_x.shape[0]
    p_x = h_x @ w_in
    p_c = h_c @ (w_in if ctx_out else w_in[:, :CTX_SIDE_COLS])

    q_x = apply_axial_rope(_heads(p_x, COL_Q, ATT_WIDTH, N_HEADS), rope)
    k_x = apply_axial_rope(_heads(p_x, COL_K, KV_WIDTH, N_KV_HEADS), rope)
    v_x = _heads(p_x, COL_V, KV_WIDTH, N_KV_HEADS)
    k_c = _heads(p_c, COL_K, KV_WIDTH, N_KV_HEADS)
    v_c = _heads(p_c, COL_V, KV_WIDTH, N_KV_HEADS)
    att_x = window_attention_latent(q_x, k_x, v_x, k_c, v_c, sink)

    zero_state = jnp.zeros((b, SSD_GROUPS, HEADS_PER_GROUP, SSD_HEAD_DIM, D_STATE), F32)
    xs_c, dt_c, bm_c, cm_c = _ssd_inputs(p_c, conv_w, conv_b)
    y_c, fin_c = ssd_bidirectional(xs_c, dt_c, bm_c, cm_c, dt_bias, a_log, d_skip,
                                   (zero_state, zero_state), ctx_out)
    xs_x, dt_x, bm_x, cm_x = _ssd_inputs(p_x, conv_w, conv_b)
    y_x, _ = ssd_bidirectional(xs_x, dt_x, bm_x, cm_x, dt_bias, a_log, d_skip, fin_c, True)

    def ssd_out(y, p):
        z = p[..., COL_Z:COL_GATE].astype(F32)
        y = y.reshape(y.shape[:2] + (D_INNER,))
        return rms_norm(y * jax.nn.silu(z), g_ssd).astype(p.dtype)

    def merge(att, ssd, p):
        gates = jax.nn.sigmoid(p[..., COL_GATE:IN_COLS].astype(F32)).astype(p.dtype)
        g_att, g_ssd_branch = gates[..., :D_MODEL], gates[..., D_MODEL:]
        return (g_att * (att @ w_o_attn) + g_ssd_branch * (ssd @ w_o_ssd)) @ w_out

    out_x = merge(att_x, ssd_out(y_x, p_x), p_x)
    out_c = None
    if ctx_out:
        q_c = _heads(p_c, COL_Q, ATT_WIDTH, N_HEADS)
        att_c = context_attention(q_c, k_c, v_c, sink)
        out_c = merge(att_c, ssd_out(y_c, p_c), p_c)
    return out_x, out_c


def setup_inputs(seed: int = 0) -> dict:
    key = jax.random.key(seed)
    ks = jax.random.split(key, 24)

    def nrm(k, shape, scale):
        return jax.random.normal(k, shape, F32) * scale

    x = nrm(ks[0], (BATCH, SEQ, D_MODEL), 1.0)
    c = nrm(ks[1], (BATCH, D_MODEL), 1.0)
    ctx = nrm(ks[2], (BATCH, CTX_LEN, D_MODEL), 1.0)
    c_ctx = nrm(ks[3], (D_MODEL,), 1.0)
    w_ada = nrm(ks[4], (DEPTH, D_MODEL, N_MOD * D_MODEL), 0.5 * D_MODEL ** -0.5)
    b_ada = nrm(ks[5], (DEPTH, N_MOD * D_MODEL), 0.02)
    g_norm1 = 1.0 + nrm(ks[6], (DEPTH, D_MODEL), 0.02)
    g_norm2 = 1.0 + nrm(ks[7], (DEPTH, D_MODEL), 0.02)
    w_in = nrm(ks[8], (DEPTH, D_MODEL, IN_COLS), D_MODEL ** -0.5)
    attn_sink = nrm(ks[9], (DEPTH, N_HEADS), 0.5)
    conv_w = nrm(ks[10], (DEPTH, CONV_K, CONV_CH), CONV_K ** -0.5)
    conv_b = nrm(ks[11], (DEPTH, CONV_CH), 0.02)
    dt0 = jnp.exp(jax.random.uniform(ks[12], (DEPTH, 2, SSD_HEADS), F32,
                                     minval=math.log(1e-3), maxval=math.log(1e-1)))
    dt_bias = dt0 + jnp.log(-jnp.expm1(-dt0))
    a_log = jnp.log(jax.random.uniform(ks[13], (DEPTH, 2, SSD_HEADS), F32, minval=1.0, maxval=16.0))
    d_skip = 1.0 + nrm(ks[14], (DEPTH, 2, SSD_HEADS), 0.1)
    g_ssd = 1.0 + nrm(ks[15], (DEPTH, D_INNER), 0.02)
    w_o_attn = nrm(ks[16], (DEPTH, ATT_WIDTH, D_MODEL), ATT_WIDTH ** -0.5)
    w_o_ssd = nrm(ks[17], (DEPTH, D_INNER, D_MODEL), D_INNER ** -0.5)
    w_out = nrm(ks[18], (DEPTH, D_MODEL, D_MODEL), D_MODEL ** -0.5)
    w_ff1 = nrm(ks[19], (DEPTH, D_MODEL, D_FF), D_MODEL ** -0.5)
    w_ff2 = nrm(ks[20], (DEPTH, D_FF, D_MODEL), D_FF ** -0.5)
    g_final = 1.0 + nrm(ks[21], (D_MODEL,), 0.02)
    return {'x': x, 'c': c, 'ctx': ctx, 'c_ctx': c_ctx, 'w_ada': w_ada, 'b_ada': b_ada,
            'g_norm1': g_norm1, 'g_norm2': g_norm2, 'w_in': w_in, 'attn_sink': attn_sink,
            'conv_w': conv_w, 'conv_b': conv_b, 'dt_bias': dt_bias, 'a_log': a_log,
            'd_skip': d_skip, 'g_ssd': g_ssd, 'w_o_attn': w_o_attn, 'w_o_ssd': w_o_ssd,
            'w_out': w_out, 'w_ff1': w_ff1, 'w_ff2': w_ff2, 'g_final': g_final}


def reference(x, c, ctx, c_ctx, w_ada, b_ada, g_norm1, g_norm2, w_in, attn_sink, conv_w, conv_b,
              dt_bias, a_log, d_skip, g_ssd, w_o_attn, w_o_ssd, w_out, w_ff1, w_ff2, g_final):
    rows = x.shape[1] // GRID_W
    rope = axial_rope_tables(rows)
    silu_c = jax.nn.silu(c)[:, None, :]
    silu_cc = jax.nn.silu(c_ctx)
    h_ctx = ctx
    for i in range(DEPTH):
        ctx_out = i < DEPTH - 1
        mod_x = jnp.split(silu_c @ w_ada[i] + b_ada[i], N_MOD, axis=-1)
        n_ctx_mod = N_MOD if ctx_out else 2
        mod_c = jnp.split(silu_cc @ w_ada[i][:, :n_ctx_mod * D_MODEL] + b_ada[i][:n_ctx_mod * D_MODEL],
                          n_ctx_mod, axis=-1)
        hx = modulate(rms_norm(x, g_norm1[i]), mod_x[0], mod_x[1])
        hc = modulate(rms_norm(h_ctx, g_norm1[i]), mod_c[0], mod_c[1])
        mix_x, mix_c = hybrid_mixer(hx, hc, rope, w_in[i], attn_sink[i], conv_w[i], conv_b[i],
                                    dt_bias[i], a_log[i], d_skip[i], g_ssd[i], w_o_attn[i],
                                    w_o_ssd[i], w_out[i], ctx_out)
        x = x + mod_x[2] * mix_x
        x = x + mod_x[5] * squared_relu_mlp(modulate(rms_norm(x, g_norm2[i]), mod_x[3], mod_x[4]),
                                            w_ff1[i], w_ff2[i])
        if ctx_out:
            h_ctx = h_ctx + mod_c[2] * mix_c
            h_ctx = h_ctx + mod_c[5] * squared_relu_mlp(
                modulate(rms_norm(h_ctx, g_norm2[i]), mod_c[3], mod_c[4]), w_ff1[i], w_ff2[i])
    return rms_norm(x, g_final)
```

```python
import functools

import jax
import jax.numpy as jnp
from jax import lax
from jax.experimental import pallas as pl
from jax.experimental.pallas import tpu as pltpu

F32 = jnp.float32
BF16 = jnp.bfloat16

D_MODEL = 2048
GRID_W = 64
N_HEADS = 16
N_KV_HEADS = 4
HEAD_DIM = 128
REP = N_HEADS // N_KV_HEADS
ATT_WIDTH = N_HEADS * HEAD_DIM
KV_WIDTH = N_KV_HEADS * HEAD_DIM
WINDOW = 128
ATT_BLOCK = 128
ROPE_THETA = 10000.0
D_INNER = 2 * D_MODEL
SSD_HEAD_DIM = 64
SSD_HEADS = D_INNER // SSD_HEAD_DIM
SSD_GROUPS = 8
HEADS_PER_GROUP = SSD_HEADS // SSD_GROUPS
D_STATE = 128
CONV_K = 5
SSD_CHUNK = 128
GROUP_WIDTH = HEADS_PER_GROUP * SSD_HEAD_DIM
CONV_CH = D_INNER + 2 * SSD_GROUPS * D_STATE
D_FF = 4 * D_MODEL
N_MOD = 6
EPS = 1e-6

REF_COL_XBC = 2 * KV_WIDTH
REF_COL_DT = REF_COL_XBC + CONV_CH
REF_COL_Q = REF_COL_DT + 2 * SSD_HEADS
REF_COL_Z = REF_COL_Q + ATT_WIDTH
REF_COL_GATE = REF_COL_Z + D_INNER

P_Z = 0
P_GATE_A = P_Z + D_INNER
P_GATE_S = P_GATE_A + D_MODEL
P_Q = P_GATE_S + D_MODEL
P_K = P_Q + ATT_WIDTH
P_V = P_K + KV_WIDTH
P_XBC = P_V + KV_WIDTH
P_BM = P_XBC + D_INNER
P_CM = P_BM + SSD_GROUPS * D_STATE
P_COLS = P_CM + SSD_GROUPS * D_STATE

LANES = 128
VMEM_LIMIT = 56 * 1024 * 1024


def _params(*sem):
    return pltpu.CompilerParams(dimension_semantics=sem, vmem_limit_bytes=VMEM_LIMIT)


def _mm_kernel(*refs, n_pairs, n_extra, epilogue):
    a_refs = refs[:n_pairs]
    w_refs = refs[n_pairs:2 * n_pairs]
    extra = refs[2 * n_pairs:2 * n_pairs + n_extra]
    o_ref = refs[2 * n_pairs + n_extra]
    parts = [jnp.dot(a[...], w[...], preferred_element_type=F32) for a, w in zip(a_refs, w_refs)]
    o_ref[...] = epilogue(parts, [e[...] for e in extra]).astype(o_ref.dtype)


def matmul(pairs, extras, epilogue, n_out, out_dtype, tm, tn, w_col0=0, name="mm"):
    m = pairs[0][0].shape[0]
    tm = min(tm, m)
    assert m % tm == 0 and n_out % tn == 0 and w_col0 % tn == 0
    jo = w_col0 // tn
    in_specs = [pl.BlockSpec((tm, a.shape[1]), lambda i, j: (i, 0)) for a, _ in pairs]
    in_specs += [pl.BlockSpec((w.shape[0], tn), lambda i, j: (0, j + jo)) for _, w in pairs]
    in_specs += [pl.BlockSpec(bs, im) for _, bs, im in extras]
    kern = functools.partial(_mm_kernel, n_pairs=len(pairs), n_extra=len(extras), epilogue=epilogue)
    return pl.pallas_call(
        kern,
        grid=(m // tm, n_out // tn),
        in_specs=in_specs,
        out_specs=pl.BlockSpec((tm, tn), lambda i, j: (i, j)),
        out_shape=jax.ShapeDtypeStruct((m, n_out), out_dtype),
        compiler_params=_params("parallel", "arbitrary"),
        name=name,
    )(*[a for a, _ in pairs], *[w for _, w in pairs], *[e for e, _, _ in extras])


def _mm_acc_kernel(*refs, nk, n_extra, epilogue):
    a_ref, w_ref = refs[0], refs[1]
    extra = refs[2:2 + n_extra]
    o_ref, acc_ref = refs[2 + n_extra], refs[3 + n_extra]
    k = pl.program_id(2)
    part = jnp.dot(a_ref[...], w_ref[...], preferred_element_type=F32)

    @pl.when(k == 0)
    def _():
        acc_ref[...] = part

    @pl.when(k > 0)
    def _():
        acc_ref[...] += part

    @pl.when(k == nk - 1)
    def _():
        o_ref[...] = epilogue([acc_ref[...]], [e[...] for e in extra]).astype(o_ref.dtype)


def matmul_ktiled(a, w, extras, epilogue, out_dtype, tm, tn, tk, name="mmk"):
    m, kdim = a.shape
    n_out = w.shape[1]
    tm = min(tm, m)
    assert m % tm == 0 and n_out % tn == 0 and kdim % tk == 0
    nk = kdim // tk
    in_specs = [pl.BlockSpec((tm, tk), lambda i, j, k: (i, k)),
                pl.BlockSpec((tk, tn), lambda i, j, k: (k, j))]
    in_specs += [pl.BlockSpec(bs, (lambda i, j, k, im=im: im(i, j))) for _, bs, im in extras]
    kern = functools.partial(_mm_acc_kernel, nk=nk, n_extra=len(extras), epilogue=epilogue)
    return pl.pallas_call(
        kern,
        grid=(m // tm, n_out // tn, nk),
        in_specs=in_specs,
        out_specs=pl.BlockSpec((tm, tn), lambda i, j, k: (i, j)),
        out_shape=jax.ShapeDtypeStruct((m, n_out), out_dtype),
        scratch_shapes=[pltpu.VMEM((tm, tn), F32)],
        compiler_params=_params("parallel", "arbitrary", "arbitrary"),
        name=name,
    )(a, w, *[e for e, _, _ in extras])


def _adaln_kernel(c_ref, w_ref, b_ref, o_ref):
    c = c_ref[...]
    silu_c = c * jax.nn.sigmoid(c)
    o_ref[...] = jnp.dot(silu_c, w_ref[...], preferred_element_type=F32,
                         precision=lax.Precision.HIGHEST) + b_ref[...]


def adaln_modulation(cond, w, b, tn=512):
    r, d = cond.shape
    n = w.shape[1]
    return pl.pallas_call(
        _adaln_kernel,
        grid=(n // tn,),
        in_specs=[pl.BlockSpec((r, d), lambda j: (0, 0)),
                  pl.BlockSpec((d, tn), lambda j: (0, j)),
                  pl.BlockSpec((1, tn), lambda j: (0, j))],
        out_specs=pl.BlockSpec((r, tn), lambda j: (0, j)),
        out_shape=jax.ShapeDtypeStruct((r, n), F32),
        compiler_params=_params("arbitrary"),
        name="adaln",
    )(cond, w, b)


def _dt_proj_kernel(h_ref, w_ref, wt_ref, o_ref, ot_ref):
    h = h_ref[...]
    o_ref[...] = jnp.dot(h, w_ref[...], preferred_element_type=F32)
    ot_ref[...] = lax.dot_general(wt_ref[...], h, (((1,), (1,)), ((), ())),
                                  preferred_element_type=F32)


def dt_projection(h, w_dt, w_dt_t, tm=1024):
    m, d = h.shape
    tm = min(tm, m)
    return pl.pallas_call(
        _dt_proj_kernel,
        grid=(m // tm,),
        in_specs=[pl.BlockSpec((tm, d), lambda i: (i, 0)),
                  pl.BlockSpec((d, LANES), lambda i: (0, 0)),
                  pl.BlockSpec((LANES, d), lambda i: (0, 0))],
        out_specs=[pl.BlockSpec((tm, LANES), lambda i: (i, 0)),
                   pl.BlockSpec((LANES, tm), lambda i: (0, i))],
        out_shape=[jax.ShapeDtypeStruct((m, LANES), F32), jax.ShapeDtypeStruct((LANES, m), F32)],
        compiler_params=_params("parallel"),
        name="dt_proj",
    )(h, w_dt, w_dt_t)


def _norm_mod_kernel(x_ref, g_ref, shift_ref, scale_ref, o_ref):
    x = x_ref[...]
    y = x * lax.rsqrt(jnp.mean(x * x, axis=-1, keepdims=True) + EPS) * g_ref[...]
    o_ref[...] = (y * (1.0 + scale_ref[...]) + shift_ref[...]).astype(o_ref.dtype)


def norm_modulate(x, g, mod, mod_row, shift_idx, ts=512):
    b, t, d = x.shape
    ts = min(ts, t)
    return pl.pallas_call(
        _norm_mod_kernel,
        grid=(b, t // ts),
        in_specs=[pl.BlockSpec((None, ts, d), lambda i, j: (i, j, 0)),
                  pl.BlockSpec((1, d), lambda i, j: (0, 0)),
                  pl.BlockSpec((None, 1, d), lambda i, j: (mod_row(i), 0, shift_idx)),
                  pl.BlockSpec((None, 1, d), lambda i, j: (mod_row(i), 0, shift_idx + 1))],
        out_specs=pl.BlockSpec((None, ts, d), lambda i, j: (i, j, 0)),
        out_shape=jax.ShapeDtypeStruct((b, t, d), BF16),
        compiler_params=_params("parallel", "parallel"),
        name="norm_mod",
    )(x, g, mod, mod)


def _final_norm_kernel(x_ref, g_ref, o_ref):
    x = x_ref[...]
    o_ref[...] = x * lax.rsqrt(jnp.mean(x * x, axis=-1, keepdims=True) + EPS) * g_ref[...]


def final_norm(x, g, ts=512):
    b, t, d = x.shape
    ts = min(ts, t)
    return pl.pallas_call(
        _final_norm_kernel,
        grid=(b, t // ts),
        in_specs=[pl.BlockSpec((None, ts, d), lambda i, j: (i, j, 0)),
                  pl.BlockSpec((1, d), lambda i, j: (0, 0))],
        out_specs=pl.BlockSpec((None, ts, d), lambda i, j: (i, j, 0)),
        out_shape=jax.ShapeDtypeStruct((b, t, d), F32),
        compiler_params=_params("parallel", "parallel"),
        name="final_norm",
    )(x, g)


def _ssd_out_kernel(y_ref, z_ref, g_ref, o_ref):
    z = z_ref[...].astype(F32)
    u = y_ref[...] * (z * jax.nn.sigmoid(z))
    o_ref[...] = (u * lax.rsqrt(jnp.mean(u * u, axis=-1, keepdims=True) + EPS)
                  * g_ref[...]).astype(o_ref.dtype)


def ssd_out_norm(y, p, g, ts=256):
    b, t, d = y.shape
    ts = min(ts, t)
    return pl.pallas_call(
        _ssd_out_kernel,
        grid=(b, t // ts),
        in_specs=[pl.BlockSpec((None, ts, d), lambda i, j: (i, j, 0)),
                  pl.BlockSpec((None, ts, d), lambda i, j: (i, j, P_Z // D_INNER)),
                  pl.BlockSpec((1, d), lambda i, j: (0, 0))],
        out_specs=pl.BlockSpec((None, ts, d), lambda i, j: (i, j, 0)),
        out_shape=jax.ShapeDtypeStruct((b, t, d), BF16),
        compiler_params=_params("parallel", "parallel"),
        name="ssd_out_norm",
    )(y, p, g)


def _conv_silu_kernel(u_ref, w_ref, b_ref, o_ref):
    u = u_ref[...].astype(F32)
    t = u.shape[0]
    row = lax.broadcasted_iota(jnp.int32, u.shape, 0)
    pad = CONV_K // 2
    acc = b_ref[...] + w_ref[pad:pad + 1, :] * u
    for k in range(CONV_K):
        off = k - pad
        if off == 0:
            continue
        shifted = pltpu.roll(u, (-off) % t, axis=0)
        valid = (row + off >= 0) & (row + off < t)
        acc = acc + w_ref[k:k + 1, :] * jnp.where(valid, shifted, 0.0)
    o_ref[...] = (acc * jax.nn.sigmoid(acc)).astype(o_ref.dtype)


def conv_silu(p, col0, conv_w, conv_b, tc=512):
    b, t, _ = p.shape
    c0 = (col0 + P_XBC) // tc
    return pl.pallas_call(
        _conv_silu_kernel,
        grid=(b, CONV_CH // tc),
        in_specs=[pl.BlockSpec((None, t, tc), lambda i, j: (i, 0, c0 + j)),
                  pl.BlockSpec((CONV_K, tc), lambda i, j: (0, j)),
                  pl.BlockSpec((1, tc), lambda i, j: (0, j))],
        out_specs=pl.BlockSpec((None, t, tc), lambda i, j: (i, 0, j)),
        out_shape=jax.ShapeDtypeStruct((b, t, CONV_CH), BF16),
        compiler_params=_params("parallel", "parallel"),
        name="conv_silu",
    )(p, conv_w, conv_b)


def _rope(u, cos, sin_next, sin_prev):
    quarter = HEAD_DIM // 4
    return (u * cos + pltpu.roll(u, HEAD_DIM - quarter, axis=1) * sin_next
            + pltpu.roll(u, quarter, axis=1) * sin_prev)


def _softmax_pv(s, sink_col, v):
    m = jnp.maximum(jnp.max(s, axis=-1, keepdims=True), sink_col)
    e = jnp.exp(s - m)
    denom = jnp.sum(e, axis=-1, keepdims=True) + jnp.exp(sink_col - m)
    o = jnp.dot(e.astype(BF16), v, preferred_element_type=F32)
    return o / denom


def _sink_column(sink_ref, h, rows):
    blk = lax.broadcasted_iota(jnp.int32, (REP * rows, 1), 0) // rows
    col = jnp.zeros((REP * rows, 1), F32)
    for r in range(REP):
        col = jnp.where(blk == r, sink_ref[h * REP + r], col)
    return col


def _latent_attn_kernel(sink_ref, q_ref, kp_ref, kc_ref, kn_ref, vp_ref, vc_ref, vn_ref,
                        kx_ref, vx_ref, qcos_ref, qsn_ref, qsp_ref, cos_ref, sn_ref, sp_ref, o_ref,
                        *, nb):
    n = pl.program_id(1)
    h = pl.program_id(2)
    blk = ATT_BLOCK

    def rows(ref, i):
        return ref[pl.ds(pl.multiple_of(i * blk, blk), blk), :]

    qcos, qsn, qsp = rows(qcos_ref, n), rows(qsn_ref, n), rows(qsp_ref, n)
    q = jnp.concatenate(
        [_rope(q_ref[:, r * HEAD_DIM:(r + 1) * HEAD_DIM].astype(F32), qcos, qsn, qsp).astype(BF16)
         for r in range(REP)], axis=0)

    def rope_k(ref, i):
        return _rope(ref[...].astype(F32), rows(cos_ref, i), rows(sn_ref, i),
                     rows(sp_ref, i)).astype(BF16)

    n_prev = jnp.maximum(n - 1, 0)
    n_next = jnp.minimum(n + 1, nb - 1)
    keys = jnp.concatenate([rope_k(kp_ref, n_prev), rope_k(kc_ref, n), rope_k(kn_ref, n_next),
                            kx_ref[...]], axis=0)
    vals = jnp.concatenate([vp_ref[...], vc_ref[...], vn_ref[...], vx_ref[...]], axis=0)
    s = lax.dot_general(q, keys, (((1,), (1,)), ((), ())), preferred_element_type=F32)
    qi = lax.broadcasted_iota(jnp.int32, s.shape, 0) % blk
    kj = lax.broadcasted_iota(jnp.int32, s.shape, 1)
    first = (n == 0).astype(jnp.int32)
    final = (n == nb - 1).astype(jnp.int32)
    lo = qi * (1 - first) + first * blk
    hi = 2 * blk + qi * (1 - final) - final
    valid = (kj >= lo) & ((kj <= hi) | (kj >= 3 * blk))
    s = jnp.where(valid, s, -jnp.inf)
    o = _softmax_pv(s, _sink_column(sink_ref, h, blk), vals)
    for r in range(REP):
        o_ref[:, r * HEAD_DIM:(r + 1) * HEAD_DIM] = o[r * blk:(r + 1) * blk].astype(o_ref.dtype)


def latent_attention(p, p_ctx, ctx_col0, sink, q_tables, k_tables):
    b, s, _ = p.shape
    l = p_ctx.shape[1]
    nb = s // ATT_BLOCK
    blk = ATT_BLOCK
    kb, vb = P_K // HEAD_DIM, P_V // HEAD_DIM
    kxb, vxb = (ctx_col0 + P_K) // HEAD_DIM, (ctx_col0 + P_V) // HEAD_DIM
    qb = P_Q // (REP * HEAD_DIM)
    prev = lambda n: jnp.maximum(n - 1, 0)
    nxt = lambda n: jnp.minimum(n + 1, nb - 1)
    kv_spec = lambda cb, f: pl.BlockSpec((None, blk, HEAD_DIM), lambda i, n, h: (i, f(n), cb + h))
    table = pl.BlockSpec((s, HEAD_DIM), lambda i, n, h: (0, 0))
    return pl.pallas_call(
        functools.partial(_latent_attn_kernel, nb=nb),
        grid=(b, nb, N_KV_HEADS),
        in_specs=[pl.BlockSpec(memory_space=pltpu.SMEM),
                  pl.BlockSpec((None, blk, REP * HEAD_DIM), lambda i, n, h: (i, n, qb + h)),
                  kv_spec(kb, prev), kv_spec(kb, lambda n: n), kv_spec(kb, nxt),
                  kv_spec(vb, prev), kv_spec(vb, lambda n: n), kv_spec(vb, nxt),
                  pl.BlockSpec((None, l, HEAD_DIM), lambda i, n, h: (i, 0, kxb + h)),
                  pl.BlockSpec((None, l, HEAD_DIM), lambda i, n, h: (i, 0, vxb + h)),
                  table, table, table, table, table, table],
        out_specs=pl.BlockSpec((None, blk, REP * HEAD_DIM), lambda i, n, h: (i, n, h)),
        out_shape=jax.ShapeDtypeStruct((b, s, ATT_WIDTH), BF16),
        compiler_params=_params("parallel", "parallel", "arbitrary"),
        name="latent_attn",
    )(sink, p, p, p, p, p, p, p, p_ctx, p_ctx, *q_tables, *k_tables)


def _context_attn_kernel(sink_ref, q_ref, k_ref, v_ref, o_ref):
    h = pl.program_id(2)
    blk = q_ref.shape[0]
    q = jnp.concatenate([q_ref[:, r * HEAD_DIM:(r + 1) * HEAD_DIM] for r in range(REP)], axis=0)
    s = lax.dot_general(q, k_ref[...], (((1,), (1,)), ((), ())), preferred_element_type=F32)
    o = _softmax_pv(s * HEAD_DIM ** -0.5, _sink_column(sink_ref, h, blk), v_ref[...])
    for r in range(REP):
        o_ref[:, r * HEAD_DIM:(r + 1) * HEAD_DIM] = o[r * blk:(r + 1) * blk].astype(o_ref.dtype)


def context_attention(p_ctx, sink):
    b, l, _ = p_ctx.shape
    blk = min(ATT_BLOCK, l)
    kb, vb = P_K // HEAD_DIM, P_V // HEAD_DIM
    qb = P_Q // (REP * HEAD_DIM)
    return pl.pallas_call(
        _context_attn_kernel,
        grid=(b, l // blk, N_KV_HEADS),
        in_specs=[pl.BlockSpec(memory_space=pltpu.SMEM),
                  pl.BlockSpec((None, blk, REP * HEAD_DIM), lambda i, n, h: (i, n, qb + h)),
                  pl.BlockSpec((None, l, HEAD_DIM), lambda i, n, h: (i, 0, kb + h)),
                  pl.BlockSpec((None, l, HEAD_DIM), lambda i, n, h: (i, 0, vb + h))],
        out_specs=pl.BlockSpec((None, blk, REP * HEAD_DIM), lambda i, n, h: (i, n, h)),
        out_shape=jax.ShapeDtypeStruct((b, l, ATT_WIDTH), BF16),
        compiler_params=_params("parallel", "parallel", "arbitrary"),
        name="context_attn",
    )(sink, p_ctx, p_ctx, p_ctx)


def _split3(x):
    hi = x.astype(BF16)
    r1 = x - hi.astype(F32)
    mid = r1.astype(BF16)
    lo = (r1 - mid.astype(F32)).astype(BF16)
    return hi, mid, lo


def _softplus(x):
    return jnp.maximum(x, 0.0) + jnp.log1p(jnp.exp(-jnp.abs(x)))


def _ssd_kernel(*refs, nc, zero_init):
    if zero_init:
        (x_ref, b_ref, c_ref, dtn_ref, dtt_ref, bias_n_ref, alog_n_ref, bias_t_ref, alog_t_ref,
         dskip_ref, y_ref, fin_ref, state_ref) = refs
        init_ref = None
    else:
        (x_ref, b_ref, c_ref, dtn_ref, dtt_ref, bias_n_ref, alog_n_ref, bias_t_ref, alog_t_ref,
         dskip_ref, init_ref, y_ref, fin_ref, state_ref) = refs
    q = SSD_CHUNK
    e_heads = HEADS_PER_GROUP
    ii = lax.broadcasted_iota(jnp.int32, (q, q), 0)
    jj = lax.broadcasted_iota(jnp.int32, (q, q), 1)
    lower = jj <= ii
    upper = jj >= ii
    lower_ones = lower.astype(BF16)
    upper_ones = upper.astype(BF16)
    left = jj < SSD_HEAD_DIM

    def pair_cols(cols, p):
        return jnp.where(left, cols[:, 2 * p:2 * p + 1], cols[:, 2 * p + 1:2 * p + 2])

    for d in range(2):
        lo, hi = d * e_heads, (d + 1) * e_heads
        bias_n = bias_n_ref[:, lo:hi]
        aneg_n = -jnp.exp(alog_n_ref[:, lo:hi])
        bias_t = bias_t_ref[lo:hi, :]
        aneg_t = -jnp.exp(alog_t_ref[lo:hi, :])
        mask = upper if d else lower
        cum_n_mat = upper_ones if d else lower_ones
        cum_t_mat = lower_ones if d else upper_ones
        last = 0 if d else q - 1
        if zero_init:
            state_ref[...] = jnp.zeros_like(state_ref)
        else:
            state_ref[...] = init_ref[d]

        def chunk(step, carry, d=d, lo=lo, hi=hi, bias_n=bias_n, aneg_n=aneg_n, bias_t=bias_t,
                  aneg_t=aneg_t, mask=mask, cum_n_mat=cum_n_mat, cum_t_mat=cum_t_mat, last=last):
            c = (nc - 1 - step) if d else step
            r0 = pl.multiple_of(c * q, q)
            rows = pl.ds(r0, q)
            xc = x_ref[rows, :].astype(F32)
            bc = b_ref[rows, :]
            cc = c_ref[rows, :]
            dt_n = _softplus(dtn_ref[rows, lo:hi] + bias_n)
            dt_t = _softplus(dtt_ref[lo:hi, rows] + bias_t)
            acum_n = sum(jnp.dot(cum_n_mat, part, preferred_element_type=F32)
                         for part in _split3(dt_n * aneg_n))
            acum_t = sum(jnp.dot(part, cum_t_mat, preferred_element_type=F32)
                         for part in _split3(dt_t * aneg_t))
            a_last = acum_n[last:last + 1, :]
            exp_a = jnp.exp(acum_n)
            exp_w = jnp.exp(a_last - acum_n)
            cb = lax.dot_general(cc, bc, (((1,), (1,)), ((), ())), preferred_element_type=F32)
            state = state_ref[...]
            y_off = jnp.dot(cc, state.astype(BF16), preferred_element_type=F32)
            w_end = []
            state_decay = []
            for p in range(e_heads // 2):
                cols = slice(p * LANES, (p + 1) * LANES)
                xp = xc[:, cols]
                xdt = xp * pair_cols(dt_n, p)
                decays = []
                for e in (2 * p, 2 * p + 1):
                    seg = acum_n[:, e:e + 1] - acum_t[e:e + 1, :]
                    decays.append((cb * jnp.exp(jnp.where(mask, seg, -jnp.inf))).astype(BF16))
                lhs = jnp.concatenate(decays, axis=1)
                rhs = jnp.concatenate([jnp.where(left, xdt, 0.0), jnp.where(left, 0.0, xdt)],
                                      axis=0).astype(BF16)
                y_diag = jnp.dot(lhs, rhs, preferred_element_type=F32)
                exp_a_p = pair_cols(exp_a, p)
                y = y_diag + y_off[:, cols] * exp_a_p + xp * dskip_ref[d, :, cols]
                if d:
                    y_ref[rows, cols] += y
                else:
                    y_ref[rows, cols] = y
                w_end.append((pair_cols(exp_w, p) * xdt).astype(BF16))
                state_decay.append(exp_a_p[last:last + 1, :])
            new = lax.dot_general(bc, jnp.concatenate(w_end, axis=1), (((0,), (0,)), ((), ())),
                                  preferred_element_type=F32)
            state_ref[...] = jnp.concatenate(state_decay, axis=1) * state + new
            return carry

        lax.fori_loop(0, nc, chunk, 0)
        fin_ref[d] = state_ref[...]


def ssd_scan(xbc, dt_n, dt_t, bias_n, alog_n, bias_t, alog_t, dskip, init):
    b, t, _ = xbc.shape
    nc = t // SSD_CHUNK
    g = SSD_GROUPS
    e2 = 2 * HEADS_PER_GROUP
    bm0 = D_INNER // D_STATE
    cm0 = bm0 + g
    state_shape = (2, D_STATE, GROUP_WIDTH)
    in_specs = [pl.BlockSpec((None, t, GROUP_WIDTH), lambda i, j: (i, 0, j)),
                pl.BlockSpec((None, t, D_STATE), lambda i, j: (i, 0, bm0 + j)),
                pl.BlockSpec((None, t, D_STATE), lambda i, j: (i, 0, cm0 + j)),
                pl.BlockSpec((None, None, t, e2), lambda i, j: (i, j, 0, 0)),
                pl.BlockSpec((e2, t), lambda i, j: (j, i)),
                pl.BlockSpec((None, 1, e2), lambda i, j: (j, 0, 0)),
                pl.BlockSpec((None, 1, e2), lambda i, j: (j, 0, 0)),
                pl.BlockSpec((None, e2, 1), lambda i, j: (j, 0, 0)),
                pl.BlockSpec((None, e2, 1), lambda i, j: (j, 0, 0)),
                pl.BlockSpec((None, 2, 1, GROUP_WIDTH), lambda i, j: (j, 0, 0, 0))]
    args = [xbc, xbc, xbc, dt_n, dt_t, bias_n, alog_n, bias_t, alog_t, dskip]
    if init is not None:
        in_specs.append(pl.BlockSpec((None, None) + state_shape, lambda i, j: (i, j, 0, 0, 0)))
        args.append(init)
    return pl.pallas_call(
        functools.partial(_ssd_kernel, nc=nc, zero_init=init is None),
        grid=(b, g),
        in_specs=in_specs,
        out_specs=[pl.BlockSpec((None, t, GROUP_WIDTH), lambda i, j: (i, 0, j)),
                   pl.BlockSpec((None, None) + state_shape, lambda i, j: (i, j, 0, 0, 0))],
        out_shape=[jax.ShapeDtypeStruct((b, t, D_INNER), F32),
                   jax.ShapeDtypeStruct((b, g) + state_shape, F32)],
        scratch_shapes=[pltpu.VMEM((D_STATE, GROUP_WIDTH), F32)],
        compiler_params=_params("parallel", "parallel"),
        name="ssd_scan",
    )(*args)


def _rope_tables(seq):
    t = jnp.arange(seq)
    row = (t // GRID_W).astype(F32)
    col = (t % GRID_W).astype(F32)
    axis_dim = HEAD_DIM // 2
    inv_freq = ROPE_THETA ** (-jnp.arange(0, axis_dim, 2, dtype=F32) / axis_dim)
    ang_r = row[:, None] * inv_freq[None]
    ang_c = col[:, None] * inv_freq[None]
    cos_r, sin_r, cos_c, sin_c = jnp.cos(ang_r), jnp.sin(ang_r), jnp.cos(ang_c), jnp.sin(ang_c)
    zero = jnp.zeros_like(sin_r)
    cos = jnp.concatenate([cos_r, cos_r, cos_c, cos_c], axis=-1)
    sin_next = jnp.concatenate([-sin_r, zero, -sin_c, zero], axis=-1)
    sin_prev = jnp.concatenate([zero, sin_r, zero, sin_c], axis=-1)
    return cos, sin_next, sin_prev


def _group_major(v):
    return v.reshape(2, SSD_GROUPS, HEADS_PER_GROUP).transpose(1, 0, 2).reshape(
        SSD_GROUPS, 2 * HEADS_PER_GROUP)


def _mixer_side(h, p_cols_lo, w_main, w_dt, w_dt_t, conv_w, conv_b, ssd_params, init):
    b, t, d = h.shape
    h2 = h.reshape(b * t, d)
    n_out = P_COLS - p_cols_lo
    p = matmul([(h2, w_main)], [], lambda parts, ex: parts[0], n_out, BF16, tm=1024, tn=1024,
               w_col0=p_cols_lo, name="in_proj").reshape(b, t, n_out)
    dt_n, dt_t = dt_projection(h2, w_dt, w_dt_t)
    dt_n = dt_n.reshape(b, t, SSD_GROUPS, 2 * HEADS_PER_GROUP).transpose(0, 2, 1, 3)
    xbc = conv_silu(p, -p_cols_lo, conv_w, conv_b)
    y, fin = ssd_scan(xbc, dt_n, dt_t, *ssd_params, init)
    return p, y, fin


def _merge_out(att, ssd, p, w_o_attn, w_o_ssd, w_out, x, mod, mod_row):
    b, t, d = x.shape
    m = b * t
    tm, tn = min(512, t), 1024
    p2 = p.reshape(m, p.shape[-1])
    ga0, gs0 = P_GATE_A // tn, P_GATE_S // tn

    def merge(parts, ex):
        ga, gs = ex
        return (jax.nn.sigmoid(ga.astype(F32)) * parts[0]
                + jax.nn.sigmoid(gs.astype(F32)) * parts[1])

    u = matmul([(att.reshape(m, ATT_WIDTH), w_o_attn), (ssd.reshape(m, D_INNER), w_o_ssd)],
               [(p2, (tm, tn), lambda i, j: (i, ga0 + j)), (p2, (tm, tn), lambda i, j: (i, gs0 + j))],
               merge, d, BF16, tm=tm, tn=tn, name="merge")
    return _residual_matmul(u, w_out, x, mod, mod_row, 2, name="out_proj")


def _residual_matmul(a, w, x, mod, mod_row, gate_idx, name):
    b, t, d = x.shape
    tm, tn = min(1024, t), 1024
    per_seq = t // tm
    nj = d // tn
    extras = [(x.reshape(b * t, d), (tm, tn), lambda i, j: (i, j)),
              (mod, (None, 1, tn), lambda i, j: (mod_row(i // per_seq), 0, gate_idx * nj + j))]
    res = lambda parts, ex: ex[0] + ex[1] * parts[0]
    if a.shape[1] > 4096:
        out = matmul_ktiled(a, w, extras, res, F32, tm=tm, tn=tn, tk=2048, name=name)
    else:
        out = matmul([(a, w)], extras, res, d, F32, tm=tm, tn=tn, name=name)
    return out.reshape(b, t, d)


def _mlp(x, g, w1, w2, mod, mod_row):
    b, t, d = x.shape
    h = norm_modulate(x, g, mod, mod_row, 3).reshape(b * t, d)
    hid = matmul([(h, w1)], [], lambda parts, ex: jnp.square(jnp.maximum(parts[0], 0.0)), D_FF,
                 BF16, tm=1024, tn=1024, name="ff1")
    return _residual_matmul(hid, w2, x, mod, mod_row, 5, name="ff2")


def kernel(x, c, ctx, c_ctx, w_ada, b_ada, g_norm1, g_norm2, w_in, attn_sink, conv_w, conv_b,
           dt_bias, a_log, d_skip, g_ssd, w_o_attn, w_o_ssd, w_out, w_ff1, w_ff2, g_final):
    depth = w_in.shape[0]
    batch, seq, d = x.shape
    q_scale = HEAD_DIM ** -0.5
    k_tables = _rope_tables(seq)
    q_tables = tuple(tb * q_scale for tb in k_tables)

    n_rows = -(-(batch + 1) // 8) * 8
    cond = jnp.zeros((n_rows, d), F32).at[:batch].set(c).at[batch].set(c_ctx)
    latent_row = lambda i: i
    ctx_row = lambda i: batch

    h_ctx = ctx
    for i in range(depth):
        ctx_out = i < depth - 1
        mod = adaln_modulation(cond, w_ada[i], b_ada[i][None]).reshape(n_rows, 1, N_MOD * d)

        wi = w_in[i]
        w_main = jnp.concatenate(
            [wi[:, REF_COL_Z:REF_COL_GATE], wi[:, REF_COL_GATE:], wi[:, REF_COL_Q:REF_COL_Z],
             wi[:, :REF_COL_DT]], axis=1).astype(BF16)
        w_dt = wi[:, REF_COL_DT:REF_COL_Q].reshape(d, 2, SSD_GROUPS, HEADS_PER_GROUP).transpose(
            0, 2, 1, 3).reshape(d, 2 * SSD_HEADS).astype(BF16)
        w_dt_t = w_dt.T
        ssd_params = (_group_major(dt_bias[i])[:, None, :], _group_major(a_log[i])[:, None, :],
                      _group_major(dt_bias[i])[:, :, None], _group_major(a_log[i])[:, :, None],
                      jnp.repeat(d_skip[i].reshape(2, SSD_GROUPS, HEADS_PER_GROUP), SSD_HEAD_DIM,
                                 axis=-1).transpose(1, 0, 2)[:, :, None, :])
        sink = attn_sink[i]
        wa, ws, wo = w_o_attn[i].astype(BF16), w_o_ssd[i].astype(BF16), w_out[i].astype(BF16)
        w1, w2 = w_ff1[i].astype(BF16), w_ff2[i].astype(BF16)
        g1, g2, gs = g_norm1[i][None], g_norm2[i][None], g_ssd[i][None]

        ctx_lo = 0 if ctx_out else P_K
        hc = norm_modulate(h_ctx, g1, mod, ctx_row, 0)
        p_c, y_c, fin_c = _mixer_side(hc, ctx_lo, w_main, w_dt, w_dt_t, conv_w[i], conv_b[i][None],
                                      ssd_params, None)
        hx = norm_modulate(x, g1, mod, latent_row, 0)
        p_x, y_x, _ = _mixer_side(hx, 0, w_main, w_dt, w_dt_t, conv_w[i], conv_b[i][None],
                                  ssd_params, fin_c)
        att_x = latent_attention(p_x, p_c, -ctx_lo, sink, q_tables, k_tables)
        x = _merge_out(att_x, ssd_out_norm(y_x, p_x, gs), p_x, wa, ws, wo, x, mod, latent_row)
        x = _mlp(x, g2, w1, w2, mod, latent_row)
        if ctx_out:
            att_c = context_attention(p_c, sink)
            h_ctx = _merge_out(att_c, ssd_out_norm(y_c, p_c, gs), p_c, wa, ws, wo, h_ctx, mod,
                               ctx_row)
            h_ctx = _mlp(h_ctx, g2, w1, w2, mod, ctx_row)
    return final_norm(x, g_final[None])
```

```python
import functools

import jax
import jax.numpy as jnp
from jax import lax
from jax.experimental import pallas as pl
from jax.experimental.pallas import tpu as pltpu

F32 = jnp.float32
BF16 = jnp.bfloat16

D_MODEL = 2048
GRID_W = 64
N_HEADS = 16
N_KV_HEADS = 4
HEAD_DIM = 128
REP = N_HEADS // N_KV_HEADS
ATT_WIDTH = N_HEADS * HEAD_DIM
KV_WIDTH = N_KV_HEADS * HEAD_DIM
WINDOW = 128
ATT_BLOCK = 128
ROPE_THETA = 10000.0
D_INNER = 2 * D_MODEL
SSD_HEAD_DIM = 64
SSD_HEADS = D_INNER // SSD_HEAD_DIM
SSD_GROUPS = 8
HEADS_PER_GROUP = SSD_HEADS // SSD_GROUPS
D_STATE = 128
CONV_K = 5
SSD_CHUNK = 128
GROUP_WIDTH = HEADS_PER_GROUP * SSD_HEAD_DIM
CONV_CH = D_INNER + 2 * SSD_GROUPS * D_STATE
D_FF = 4 * D_MODEL
N_MOD = 6
EPS = 1e-6
LOG2E = 1.4426950408889634
NEG_BIG = -1e30
LOG2_DT_FLOOR = -1e4

REF_COL_XBC = 2 * KV_WIDTH
REF_COL_DT = REF_COL_XBC + CONV_CH
REF_COL_Q = REF_COL_DT + 2 * SSD_HEADS
REF_COL_Z = REF_COL_Q + ATT_WIDTH
REF_COL_GATE = REF_COL_Z + D_INNER

P_Z = 0
P_GATE_A = P_Z + D_INNER
P_GATE_S = P_GATE_A + D_MODEL
P_Q = P_GATE_S + D_MODEL
P_K = P_Q + ATT_WIDTH
P_V = P_K + KV_WIDTH
P_XBC = P_V + KV_WIDTH
P_BM = P_XBC + D_INNER
P_CM = P_BM + SSD_GROUPS * D_STATE
P_COLS = P_CM + SSD_GROUPS * D_STATE

LANES = 128
VMEM_LIMIT = 56 * 1024 * 1024


def _params(*sem):
    return pltpu.CompilerParams(dimension_semantics=sem, vmem_limit_bytes=VMEM_LIMIT)


def _mm_kernel(*refs, n_pairs, n_extra, epilogue):
    a_refs = refs[:n_pairs]
    w_refs = refs[n_pairs:2 * n_pairs]
    extra = refs[2 * n_pairs:2 * n_pairs + n_extra]
    o_ref = refs[2 * n_pairs + n_extra]
    parts = [jnp.dot(a[...], w[...], preferred_element_type=F32) for a, w in zip(a_refs, w_refs)]
    o_ref[...] = epilogue(parts, [e[...] for e in extra]).astype(o_ref.dtype)


def matmul(pairs, extras, epilogue, n_out, out_dtype, tm, tn, w_col0=0, name="mm"):
    m = pairs[0][0].shape[0]
    tm = min(tm, m)
    assert m % tm == 0 and n_out % tn == 0 and w_col0 % tn == 0
    jo = w_col0 // tn
    in_specs = [pl.BlockSpec((tm, a.shape[1]), lambda i, j: (i, 0)) for a, _ in pairs]
    in_specs += [pl.BlockSpec((w.shape[0], tn), lambda i, j: (0, j + jo)) for _, w in pairs]
    in_specs += [pl.BlockSpec(bs, im) for _, bs, im in extras]
    kern = functools.partial(_mm_kernel, n_pairs=len(pairs), n_extra=len(extras), epilogue=epilogue)
    return pl.pallas_call(
        kern,
        grid=(m // tm, n_out // tn),
        in_specs=in_specs,
        out_specs=pl.BlockSpec((tm, tn), lambda i, j: (i, j)),
        out_shape=jax.ShapeDtypeStruct((m, n_out), out_dtype),
        compiler_params=_params("parallel", "arbitrary"),
        name=name,
    )(*[a for a, _ in pairs], *[w for _, w in pairs], *[e for e, _, _ in extras])


def _mm_acc_kernel(*refs, nk, n_extra, epilogue):
    a_ref, w_ref = refs[0], refs[1]
    extra = refs[2:2 + n_extra]
    o_ref, acc_ref = refs[2 + n_extra], refs[3 + n_extra]
    k = pl.program_id(2)
    part = jnp.dot(a_ref[...], w_ref[...], preferred_element_type=F32)

    @pl.when(k == 0)
    def _():
        acc_ref[...] = part

    @pl.when(k > 0)
    def _():
        acc_ref[...] += part

    @pl.when(k == nk - 1)
    def _():
        o_ref[...] = epilogue([acc_ref[...]], [e[...] for e in extra]).astype(o_ref.dtype)


def matmul_ktiled(a, w, extras, epilogue, out_dtype, tm, tn, tk, name="mmk"):
    m, kdim = a.shape
    n_out = w.shape[1]
    tm = min(tm, m)
    assert m % tm == 0 and n_out % tn == 0 and kdim % tk == 0
    nk = kdim // tk
    in_specs = [pl.BlockSpec((tm, tk), lambda i, j, k: (i, k)),
                pl.BlockSpec((tk, tn), lambda i, j, k: (k, j))]
    in_specs += [pl.BlockSpec(bs, (lambda i, j, k, im=im: im(i, j))) for _, bs, im in extras]
    kern = functools.partial(_mm_acc_kernel, nk=nk, n_extra=len(extras), epilogue=epilogue)
    return pl.pallas_call(
        kern,
        grid=(m // tm, n_out // tn, nk),
        in_specs=in_specs,
        out_specs=pl.BlockSpec((tm, tn), lambda i, j, k: (i, j)),
        out_shape=jax.ShapeDtypeStruct((m, n_out), out_dtype),
        scratch_shapes=[pltpu.VMEM((tm, tn), F32)],
        compiler_params=_params("parallel", "arbitrary", "arbitrary"),
        name=name,
    )(a, w, *[e for e, _, _ in extras])


def _adaln_kernel(c_ref, w_ref, b_ref, o_ref):
    c = c_ref[...]
    silu_c = c * jax.nn.sigmoid(c)
    o_ref[...] = jnp.dot(silu_c, w_ref[...], preferred_element_type=F32,
                         precision=lax.Precision.HIGHEST) + b_ref[...]


def adaln_modulation(cond, w, b, tn=512):
    r, d = cond.shape
    n = w.shape[1]
    return pl.pallas_call(
        _adaln_kernel,
        grid=(n // tn,),
        in_specs=[pl.BlockSpec((r, d), lambda j: (0, 0)),
                  pl.BlockSpec((d, tn), lambda j: (0, j)),
                  pl.BlockSpec((1, tn), lambda j: (0, j))],
        out_specs=pl.BlockSpec((r, tn), lambda j: (0, j)),
        out_shape=jax.ShapeDtypeStruct((r, n), F32),
        compiler_params=_params("arbitrary"),
        name="adaln",
    )(cond, w, b)


def _dt_proj_kernel(h_ref, wt_ref, ot_ref):
    ot_ref[...] = lax.dot_general(wt_ref[...], h_ref[...], (((1,), (1,)), ((), ())),
                                  preferred_element_type=F32)


def dt_projection(h, w_dt_t, tm=1024):
    m, d = h.shape
    tm = min(tm, m)
    return pl.pallas_call(
        _dt_proj_kernel,
        grid=(m // tm,),
        in_specs=[pl.BlockSpec((tm, d), lambda i: (i, 0)),
                  pl.BlockSpec((LANES, d), lambda i: (0, 0))],
        out_specs=pl.BlockSpec((LANES, tm), lambda i: (0, i)),
        out_shape=jax.ShapeDtypeStruct((LANES, m), F32),
        compiler_params=_params("parallel"),
        name="dt_proj",
    )(h, w_dt_t)


def _norm_mod_kernel(x_ref, g_ref, shift_ref, scale_ref, o_ref):
    x = x_ref[...]
    y = x * lax.rsqrt(jnp.mean(x * x, axis=-1, keepdims=True) + EPS) * g_ref[...]
    o_ref[...] = (y * (1.0 + scale_ref[...]) + shift_ref[...]).astype(o_ref.dtype)


def norm_modulate(x, g, mod, mod_row, shift_idx, ts=512):
    b, t, d = x.shape
    ts = min(ts, t)
    return pl.pallas_call(
        _norm_mod_kernel,
        grid=(b, t // ts),
        in_specs=[pl.BlockSpec((None, ts, d), lambda i, j: (i, j, 0)),
                  pl.BlockSpec((1, d), lambda i, j: (0, 0)),
                  pl.BlockSpec((None, 1, d), lambda i, j: (mod_row(i), 0, shift_idx)),
                  pl.BlockSpec((None, 1, d), lambda i, j: (mod_row(i), 0, shift_idx + 1))],
        out_specs=pl.BlockSpec((None, ts, d), lambda i, j: (i, j, 0)),
        out_shape=jax.ShapeDtypeStruct((b, t, d), BF16),
        compiler_params=_params("parallel", "parallel"),
        name="norm_mod",
    )(x, g, mod, mod)


def _final_norm_kernel(x_ref, g_ref, o_ref):
    x = x_ref[...]
    o_ref[...] = x * lax.rsqrt(jnp.mean(x * x, axis=-1, keepdims=True) + EPS) * g_ref[...]


def final_norm(x, g, ts=512):
    b, t, d = x.shape
    ts = min(ts, t)
    return pl.pallas_call(
        _final_norm_kernel,
        grid=(b, t // ts),
        in_specs=[pl.BlockSpec((None, ts, d), lambda i, j: (i, j, 0)),
                  pl.BlockSpec((1, d), lambda i, j: (0, 0))],
        out_specs=pl.BlockSpec((None, ts, d), lambda i, j: (i, j, 0)),
        out_shape=jax.ShapeDtypeStruct((b, t, d), F32),
        compiler_params=_params("parallel", "parallel"),
        name="final_norm",
    )(x, g)


def _ssd_out_kernel(y_ref, xs_ref, z_ref, dskip_ref, g_ref, o_ref):
    z = z_ref[...].astype(F32)
    y = y_ref[...] + xs_ref[...].astype(F32) * dskip_ref[...]
    u = y * (z * jax.nn.sigmoid(z))
    o_ref[...] = (u * lax.rsqrt(jnp.mean(u * u, axis=-1, keepdims=True) + EPS)
                  * g_ref[...]).astype(o_ref.dtype)


def ssd_out_norm(y, xbc, p, dskip, g, ts=256):
    b, t, d = y.shape
    ts = min(ts, t)
    return pl.pallas_call(
        _ssd_out_kernel,
        grid=(b, t // ts),
        in_specs=[pl.BlockSpec((None, ts, d), lambda i, j: (i, j, 0)),
                  pl.BlockSpec((None, ts, d), lambda i, j: (i, j, 0)),
                  pl.BlockSpec((None, ts, d), lambda i, j: (i, j, P_Z // D_INNER)),
                  pl.BlockSpec((1, d), lambda i, j: (0, 0)),
                  pl.BlockSpec((1, d), lambda i, j: (0, 0))],
        out_specs=pl.BlockSpec((None, ts, d), lambda i, j: (i, j, 0)),
        out_shape=jax.ShapeDtypeStruct((b, t, d), BF16),
        compiler_params=_params("parallel", "parallel"),
        name="ssd_out_norm",
    )(y, xbc, p, dskip, g)


def _conv_silu_kernel(u_ref, w_ref, b_ref, o_ref):
    u = u_ref[...].astype(F32)
    t = u.shape[0]
    row = lax.broadcasted_iota(jnp.int32, u.shape, 0)
    pad = CONV_K // 2
    acc = b_ref[...] + w_ref[pad:pad + 1, :] * u
    for k in range(CONV_K):
        off = k - pad
        if off == 0:
            continue
        shifted = pltpu.roll(u, (-off) % t, axis=0)
        valid = (row + off >= 0) & (row + off < t)
        acc = acc + w_ref[k:k + 1, :] * jnp.where(valid, shifted, 0.0)
    o_ref[...] = (acc * jax.nn.sigmoid(acc)).astype(o_ref.dtype)


def conv_silu(p, col0, conv_w, conv_b, tc=512):
    b, t, _ = p.shape
    c0 = (col0 + P_XBC) // tc
    return pl.pallas_call(
        _conv_silu_kernel,
        grid=(b, CONV_CH // tc),
        in_specs=[pl.BlockSpec((None, t, tc), lambda i, j: (i, 0, c0 + j)),
                  pl.BlockSpec((CONV_K, tc), lambda i, j: (0, j)),
                  pl.BlockSpec((1, tc), lambda i, j: (0, j))],
        out_specs=pl.BlockSpec((None, t, tc), lambda i, j: (i, 0, j)),
        out_shape=jax.ShapeDtypeStruct((b, t, CONV_CH), BF16),
        compiler_params=_params("parallel", "parallel"),
        name="conv_silu",
    )(p, conv_w, conv_b)


def _rope(u, cos, sin_next, sin_prev):
    quarter = HEAD_DIM // 4
    return (u * cos + pltpu.roll(u, HEAD_DIM - quarter, axis=1) * sin_next
            + pltpu.roll(u, quarter, axis=1) * sin_prev)


def _softmax_pv(s, sink_col, v):
    m = jnp.maximum(jnp.max(s, axis=-1, keepdims=True), sink_col)
    e = jnp.exp(s - m)
    denom = jnp.sum(e, axis=-1, keepdims=True) + jnp.exp(sink_col - m)
    o = jnp.dot(e.astype(BF16), v, preferred_element_type=F32)
    return o / denom


def _sink_column(sink_ref, h, rows):
    blk = lax.broadcasted_iota(jnp.int32, (REP * rows, 1), 0) // rows
    col = jnp.zeros((REP * rows, 1), F32)
    for r in range(REP):
        col = jnp.where(blk == r, sink_ref[h * REP + r], col)
    return col


def _latent_attn_kernel(sink_ref, q_ref, kp_ref, kc_ref, kn_ref, vp_ref, vc_ref, vn_ref,
                        kx_ref, vx_ref, qcos_ref, qsn_ref, qsp_ref, cos_ref, sn_ref, sp_ref, o_ref,
                        *, nb):
    n = pl.program_id(1)
    blk = ATT_BLOCK

    def rows(ref, i):
        return ref[pl.ds(pl.multiple_of(i * blk, blk), blk), :]

    n_prev = jnp.maximum(n - 1, 0)
    n_next = jnp.minimum(n + 1, nb - 1)
    k_tables = [[rows(ref, i) for ref in (cos_ref, sn_ref, sp_ref)] for i in (n_prev, n, n_next)]
    qcos, qsn, qsp = rows(qcos_ref, n), rows(qsn_ref, n), rows(qsp_ref, n)
    qi = lax.broadcasted_iota(jnp.int32, (blk, blk), 0)
    kj = lax.broadcasted_iota(jnp.int32, (blk, blk), 1)
    see_prev = kj >= qi + (n == 0).astype(jnp.int32) * blk
    see_next = kj <= qi - (n == nb - 1).astype(jnp.int32) * blk
    for h in range(N_KV_HEADS):
        kv = slice(h * HEAD_DIM, (h + 1) * HEAD_DIM)
        keys = jnp.concatenate(
            [_rope(ref[:, kv].astype(F32), *tabs).astype(BF16)
             for ref, tabs in zip((kp_ref, kc_ref, kn_ref), k_tables)] + [kx_ref[:, kv]], axis=0)
        vals = jnp.concatenate([vp_ref[:, kv], vc_ref[:, kv], vn_ref[:, kv], vx_ref[:, kv]], axis=0)
        heads = [slice((h * REP + r) * HEAD_DIM, (h * REP + r + 1) * HEAD_DIM) for r in range(REP)]
        logits = []
        for cols in heads:
            q = _rope(q_ref[:, cols].astype(F32), qcos, qsn, qsp).astype(BF16)
            logits.append(lax.dot_general(q, keys, (((1,), (1,)), ((), ())),
                                          preferred_element_type=F32))
        probs, denoms = [], []
        for r, s in enumerate(logits):
            s = jnp.concatenate(
                [jnp.where(see_prev, s[:, :blk], -jnp.inf), s[:, blk:2 * blk],
                 jnp.where(see_next, s[:, 2 * blk:3 * blk], -jnp.inf), s[:, 3 * blk:]], axis=1)
            sink = sink_ref[h * REP + r] * LOG2E
            m = jnp.maximum(jnp.max(s, axis=-1, keepdims=True), sink)
            e = jnp.exp2(s - m)
            denoms.append(jnp.sum(e, axis=-1, keepdims=True) + jnp.exp2(sink - m))
            probs.append(e.astype(BF16))
        for cols, e, denom in zip(heads, probs, denoms):
            o = jnp.dot(e, vals, preferred_element_type=F32)
            o_ref[:, cols] = (o / denom).astype(o_ref.dtype)


def latent_attention(p, p_ctx, ctx_col0, sink, q_tables, k_tables):
    b, s, _ = p.shape
    l = p_ctx.shape[1]
    nb = s // ATT_BLOCK
    blk = ATT_BLOCK
    kb, vb = P_K // KV_WIDTH, P_V // KV_WIDTH
    kxb, vxb = (ctx_col0 + P_K) // KV_WIDTH, (ctx_col0 + P_V) // KV_WIDTH
    prev = lambda n: jnp.maximum(n - 1, 0)
    nxt = lambda n: jnp.minimum(n + 1, nb - 1)
    kv_spec = lambda cb, f: pl.BlockSpec((None, blk, KV_WIDTH), lambda i, n: (i, f(n), cb))
    table = pl.BlockSpec((s, HEAD_DIM), lambda i, n: (0, 0))
    return pl.pallas_call(
        functools.partial(_latent_attn_kernel, nb=nb),
        grid=(b, nb),
        in_specs=[pl.BlockSpec(memory_space=pltpu.SMEM),
                  pl.BlockSpec((None, blk, ATT_WIDTH), lambda i, n: (i, n, P_Q // ATT_WIDTH)),
                  kv_spec(kb, prev), kv_spec(kb, lambda n: n), kv_spec(kb, nxt),
                  kv_spec(vb, prev), kv_spec(vb, lambda n: n), kv_spec(vb, nxt),
                  pl.BlockSpec((None, l, KV_WIDTH), lambda i, n: (i, 0, kxb)),
                  pl.BlockSpec((None, l, KV_WIDTH), lambda i, n: (i, 0, vxb)),
                  table, table, table, table, table, table],
        out_specs=pl.BlockSpec((None, blk, ATT_WIDTH), lambda i, n: (i, n, 0)),
        out_shape=jax.ShapeDtypeStruct((b, s, ATT_WIDTH), BF16),
        compiler_params=_params("parallel", "parallel"),
        name="latent_attn",
    )(sink, p, p, p, p, p, p, p, p_ctx, p_ctx, *q_tables, *k_tables)


def _context_attn_kernel(sink_ref, q_ref, k_ref, v_ref, o_ref):
    h = pl.program_id(2)
    blk = q_ref.shape[0]
    q = jnp.concatenate([q_ref[:, r * HEAD_DIM:(r + 1) * HEAD_DIM] for r in range(REP)], axis=0)
    s = lax.dot_general(q, k_ref[...], (((1,), (1,)), ((), ())), preferred_element_type=F32)
    o = _softmax_pv(s * HEAD_DIM ** -0.5, _sink_column(sink_ref, h, blk), v_ref[...])
    for r in range(REP):
        o_ref[:, r * HEAD_DIM:(r + 1) * HEAD_DIM] = o[r * blk:(r + 1) * blk].astype(o_ref.dtype)


def context_attention(p_ctx, sink):
    b, l, _ = p_ctx.shape
    blk = min(ATT_BLOCK, l)
    kb, vb = P_K // HEAD_DIM, P_V // HEAD_DIM
    qb = P_Q // (REP * HEAD_DIM)
    return pl.pallas_call(
        _context_attn_kernel,
        grid=(b, l // blk, N_KV_HEADS),
        in_specs=[pl.BlockSpec(memory_space=pltpu.SMEM),
                  pl.BlockSpec((None, blk, REP * HEAD_DIM), lambda i, n, h: (i, n, qb + h)),
                  pl.BlockSpec((None, l, HEAD_DIM), lambda i, n, h: (i, 0, kb + h)),
                  pl.BlockSpec((None, l, HEAD_DIM), lambda i, n, h: (i, 0, vb + h))],
        out_specs=pl.BlockSpec((None, blk, REP * HEAD_DIM), lambda i, n, h: (i, n, h)),
        out_shape=jax.ShapeDtypeStruct((b, l, ATT_WIDTH), BF16),
        compiler_params=_params("parallel", "parallel", "arbitrary"),
        name="context_attn",
    )(sink, p_ctx, p_ctx, p_ctx)


def _split3(x):
    hi = x.astype(BF16).astype(F32)
    r1 = x - hi
    mid = r1.astype(BF16).astype(F32)
    lo = (r1 - mid).astype(BF16).astype(F32)
    return hi, mid, lo


def _softplus(x):
    return jnp.maximum(x, 0.0) + jnp.log1p(jnp.exp(-jnp.abs(x)))


def _ssd_kernel(*refs, nc, zero_init):
    state_ref, rhs_ref, decay_ref, exit_ref, entry_ref = refs[-5:]
    if zero_init:
        x_ref, b_ref, c_ref, dtt_ref, bias_ref, alog_ref, y_ref, fin_ref = refs[:-5]
        init_ref = None
    else:
        x_ref, b_ref, c_ref, dtt_ref, bias_ref, alog_ref, init_ref, y_ref, fin_ref = refs[:-5]
    q = SSD_CHUNK
    e_heads = HEADS_PER_GROUP
    wide = e_heads * q
    ii = lax.broadcasted_iota(jnp.int32, (q, q), 0)
    jj = lax.broadcasted_iota(jnp.int32, (q, q), 1)
    eye = (ii == jj).astype(BF16)
    left_b = (jj < SSD_HEAD_DIM).astype(BF16)
    right_b = (jj >= SSD_HEAD_DIM).astype(BF16)
    r32 = lax.broadcasted_iota(jnp.int32, (32, wide), 0)
    c32 = lax.broadcasted_iota(jnp.int32, (32, wide), 1) // q
    head_rows = ((r32 < 3 * e_heads) & ((r32 & (e_heads - 1)) == c32)).astype(F32)
    r8 = lax.broadcasted_iota(jnp.int32, (e_heads, wide), 0)
    c8 = lax.broadcasted_iota(jnp.int32, (e_heads, wide), 1) // q
    own_block = (r8 == c8).astype(F32)
    re = lax.broadcasted_iota(jnp.int32, (q, GROUP_WIDTH), 0)
    ce = lax.broadcasted_iota(jnp.int32, (q, GROUP_WIDTH), 1) // SSD_HEAD_DIM
    expand = ((re >= 64) & (re < 64 + 3 * e_heads) & ((re & (e_heads - 1)) == ce)).astype(BF16)
    zeros8 = jnp.zeros((e_heads, q), F32)
    ones24 = jnp.ones((3 * e_heads, q), F32)

    head_params = []
    for d in range(2):
        lo, hi = d * e_heads, (d + 1) * e_heads
        visible = (jj >= ii) if d else (jj <= ii)
        cum_mat = ((jj <= ii) if d else (jj >= ii)).astype(BF16)
        head_params.append((bias_ref[lo:hi, :], -jnp.exp(alog_ref[lo:hi, :]), cum_mat))
        rhs_ref[d, 0:q, :] = jnp.tile(jnp.where(visible, 0.0, NEG_BIG), (1, e_heads)).astype(BF16)
        rhs_ref[d, q:q + 32, :] = head_rows.astype(BF16)
        rhs_ref[d, q + 32:2 * q, :] = jnp.zeros((q - 32, wide), BF16)
        if zero_init:
            state_ref[d] = jnp.zeros(state_ref.shape[1:], F32)
        else:
            state_ref[d] = init_ref[d]

    def chunk_rows(d, step):
        c = (nc - 1 - step) if d else step
        return pl.ds(pl.multiple_of(c * q, q), q)

    def prepare_cumsum(d, step):
        lo, hi = d * e_heads, (d + 1) * e_heads
        bias_t, aneg_t, cum_mat = head_params[d]
        rows = chunk_rows(d, step)
        dt = _softplus(dtt_ref[lo:hi, rows] + bias_t)
        da = jnp.concatenate(_split3(dt * aneg_t), axis=0).astype(BF16)
        return dict(d=d, rows=rows, dt=dt, cs=jnp.dot(da, cum_mat, preferred_element_type=F32))

    def prepare_matmuls(v):
        d, dt, cs = v["d"], v["dt"], v["cs"]
        last = 0 if d else q - 1
        bc = b_ref[v["rows"], :]
        cc = c_ref[v["rows"], :]
        a2 = (cs[0:e_heads] + cs[e_heads:2 * e_heads] + cs[2 * e_heads:]) * LOG2E
        ea = jnp.exp2(a2)
        w_exit = jnp.exp2(a2[:, last:last + 1] - a2) * dt
        g = jnp.maximum(jnp.log2(dt), LOG2_DT_FLOOR) - a2
        rhs_ref[d, q + 32:q + 64, :] = jnp.concatenate(
            [jnp.tile(part, (1, e_heads)) * own_block for part in _split3(g)]
            + [jnp.zeros((e_heads, wide), F32)], axis=0).astype(BF16)
        split_t = jnp.concatenate(list(_split3(a2)) + [zeros8, ones24, zeros8]
                                  + list(_split3(ea)) + [jnp.zeros((q - 88, q), F32)], axis=0)
        split_n = split_t.T.astype(BF16)
        v["seg"] = jnp.dot(jnp.concatenate([eye, split_n], axis=1), rhs_ref[d],
                           preferred_element_type=F32)
        v["ea_wide"] = jnp.dot(split_n, expand, preferred_element_type=F32)
        v["cb"] = lax.dot_general(cc, bc, (((1,), (1,)), ((), ())), preferred_element_type=F32)
        b_t = bc.astype(F32).T
        v["b_exit"] = jnp.concatenate([b_t * w_exit[e:e + 1, :] for e in range(e_heads)],
                                      axis=1).astype(BF16)

    def prepare_store(v, slot):
        d = v["d"]
        decay_ref[slot, d] = (jnp.tile(v["cb"], (1, e_heads)) * jnp.exp2(v["seg"])).astype(BF16)
        exit_ref[slot, d] = v["b_exit"]
        entry_ref[slot, d] = v["ea_wide"]

    def emit_entry(d, step):
        rows = chunk_rows(d, step)
        state = state_ref[d]
        y_off = jnp.dot(c_ref[rows, :], state.astype(BF16), preferred_element_type=F32)
        return dict(d=d, rows=rows, state=state, y_off=y_off)

    def emit_outputs(w, slot):
        d, rows = w["d"], w["rows"]
        last = 0 if d else q - 1
        xc = x_ref[rows, :]
        for p in range(e_heads // 2):
            cols = slice(p * LANES, (p + 1) * LANES)
            two = slice(2 * p * q, 2 * (p + 1) * q)
            xp = xc[:, cols]
            x_pair = jnp.concatenate([xp * left_b, xp * right_b], axis=0)
            lhs = jnp.concatenate([decay_ref[slot, d, :, two], exit_ref[slot, d, :, two]], axis=0)
            out = jnp.dot(lhs, x_pair, preferred_element_type=F32)
            entry = entry_ref[slot, d, :, cols]
            y_ref[rows, cols] += out[:q] + w["y_off"][:, cols] * entry
            state_ref[d, :, cols] = w["state"][:, cols] * entry[last:last + 1, :] + out[q:]

    def prepare_all(step, slot):
        chains = [prepare_cumsum(d, step) for d in range(2)]
        for v in chains:
            prepare_matmuls(v)
        for v in chains:
            prepare_store(v, slot)

    def body(it, carry):
        for half in range(2):
            step = 2 * it + half
            ahead = jnp.minimum(step + 1, nc - 1)
            chains = [prepare_cumsum(d, ahead) for d in range(2)]
            entries = [emit_entry(d, step) for d in range(2)]
            for v, w in zip(chains, entries):
                emit_outputs(w, half)
                prepare_matmuls(v)
            for v in chains:
                prepare_store(v, 1 - half)
        return carry

    y_ref[...] = jnp.zeros_like(y_ref)
    prepare_all(0, 0)
    lax.fori_loop(0, nc // 2, body, 0)
    for d in range(2):
        fin_ref[d] = state_ref[d]


def ssd_scan(xbc, dt_t, bias_t, alog_t, init):
    b, t, _ = xbc.shape
    nc = t // SSD_CHUNK
    assert t % (2 * SSD_CHUNK) == 0
    wide = HEADS_PER_GROUP * SSD_CHUNK
    g = SSD_GROUPS
    e2 = 2 * HEADS_PER_GROUP
    bm0 = D_INNER // D_STATE
    cm0 = bm0 + g
    state_shape = (2, D_STATE, GROUP_WIDTH)
    in_specs = [pl.BlockSpec((None, t, GROUP_WIDTH), lambda i, j: (i, 0, j)),
                pl.BlockSpec((None, t, D_STATE), lambda i, j: (i, 0, bm0 + j)),
                pl.BlockSpec((None, t, D_STATE), lambda i, j: (i, 0, cm0 + j)),
                pl.BlockSpec((e2, t), lambda i, j: (j, i)),
                pl.BlockSpec((None, e2, 1), lambda i, j: (j, 0, 0)),
                pl.BlockSpec((None, e2, 1), lambda i, j: (j, 0, 0))]
    args = [xbc, xbc, xbc, dt_t, bias_t, alog_t]
    if init is not None:
        in_specs.append(pl.BlockSpec((None, None) + state_shape, lambda i, j: (i, j, 0, 0, 0)))
        args.append(init)
    return pl.pallas_call(
        functools.partial(_ssd_kernel, nc=nc, zero_init=init is None),
        grid=(b, g),
        in_specs=in_specs,
        out_specs=[pl.BlockSpec((None, t, GROUP_WIDTH), lambda i, j: (i, 0, j)),
                   pl.BlockSpec((None, None) + state_shape, lambda i, j: (i, j, 0, 0, 0))],
        out_shape=[jax.ShapeDtypeStruct((b, t, D_INNER), F32),
                   jax.ShapeDtypeStruct((b, g) + state_shape, F32)],
        scratch_shapes=[pltpu.VMEM(state_shape, F32),
                        pltpu.VMEM((2, 2 * SSD_CHUNK, wide), BF16),
                        pltpu.VMEM((2, 2, SSD_CHUNK, wide), BF16),
                        pltpu.VMEM((2, 2, D_STATE, wide), BF16),
                        pltpu.VMEM((2, 2, SSD_CHUNK, GROUP_WIDTH), F32)],
        compiler_params=_params("parallel", "parallel"),
        name="ssd_scan",
    )(*args)


def _rope_tables(seq):
    t = jnp.arange(seq)
    row = (t // GRID_W).astype(F32)
    col = (t % GRID_W).astype(F32)
    axis_dim = HEAD_DIM // 2
    inv_freq = ROPE_THETA ** (-jnp.arange(0, axis_dim, 2, dtype=F32) / axis_dim)
    ang_r = row[:, None] * inv_freq[None]
    ang_c = col[:, None] * inv_freq[None]
    cos_r, sin_r, cos_c, sin_c = jnp.cos(ang_r), jnp.sin(ang_r), jnp.cos(ang_c), jnp.sin(ang_c)
    zero = jnp.zeros_like(sin_r)
    cos = jnp.concatenate([cos_r, cos_r, cos_c, cos_c], axis=-1)
    sin_next = jnp.concatenate([-sin_r, zero, -sin_c, zero], axis=-1)
    sin_prev = jnp.concatenate([zero, sin_r, zero, sin_c], axis=-1)
    return cos, sin_next, sin_prev


def _group_major(v):
    return v.reshape(2, SSD_GROUPS, HEADS_PER_GROUP).transpose(1, 0, 2).reshape(
        SSD_GROUPS, 2 * HEADS_PER_GROUP)


def _mixer_side(h, p_cols_lo, w_main, w_dt_t, conv_w, conv_b, ssd_params, init):
    b, t, d = h.shape
    h2 = h.reshape(b * t, d)
    n_out = P_COLS - p_cols_lo
    p = matmul([(h2, w_main)], [], lambda parts, ex: parts[0], n_out, BF16, tm=1024, tn=1024,
               w_col0=p_cols_lo, name="in_proj").reshape(b, t, n_out)
    xbc = conv_silu(p, -p_cols_lo, conv_w, conv_b)
    y, fin = ssd_scan(xbc, dt_projection(h2, w_dt_t), *ssd_params, init)
    return p, xbc, y, fin


def _merge_out(att, ssd, p, w_o_attn, w_o_ssd, w_out, x, mod, mod_row):
    b, t, d = x.shape
    m = b * t
    tm, tn = min(512, t), 1024
    p2 = p.reshape(m, p.shape[-1])
    ga0, gs0 = P_GATE_A // tn, P_GATE_S // tn

    def merge(parts, ex):
        ga, gs = ex
        return (jax.nn.sigmoid(ga.astype(F32)) * parts[0]
                + jax.nn.sigmoid(gs.astype(F32)) * parts[1])

    u = matmul([(att.reshape(m, ATT_WIDTH), w_o_attn), (ssd.reshape(m, D_INNER), w_o_ssd)],
               [(p2, (tm, tn), lambda i, j: (i, ga0 + j)), (p2, (tm, tn), lambda i, j: (i, gs0 + j))],
               merge, d, BF16, tm=tm, tn=tn, name="merge")
    return _residual_matmul(u, w_out, x, mod, mod_row, 2, name="out_proj")


def _residual_matmul(a, w, x, mod, mod_row, gate_idx, name):
    b, t, d = x.shape
    tm, tn = min(1024, t), 1024
    per_seq = t // tm
    nj = d // tn
    extras = [(x.reshape(b * t, d), (tm, tn), lambda i, j: (i, j)),
              (mod, (None, 1, tn), lambda i, j: (mod_row(i // per_seq), 0, gate_idx * nj + j))]
    res = lambda parts, ex: ex[0] + ex[1] * parts[0]
    if a.shape[1] > 4096:
        out = matmul_ktiled(a, w, extras, res, F32, tm=tm, tn=tn, tk=2048, name=name)
    else:
        out = matmul([(a, w)], extras, res, d, F32, tm=tm, tn=tn, name=name)
    return out.reshape(b, t, d)


def _mlp(x, g, w1, w2, mod, mod_row):
    b, t, d = x.shape
    h = norm_modulate(x, g, mod, mod_row, 3).reshape(b * t, d)
    hid = matmul([(h, w1)], [], lambda parts, ex: jnp.square(jnp.maximum(parts[0], 0.0)), D_FF,
                 BF16, tm=1024, tn=1024, name="ff1")
    return _residual_matmul(hid, w2, x, mod, mod_row, 5, name="ff2")


def kernel(x, c, ctx, c_ctx, w_ada, b_ada, g_norm1, g_norm2, w_in, attn_sink, conv_w, conv_b,
           dt_bias, a_log, d_skip, g_ssd, w_o_attn, w_o_ssd, w_out, w_ff1, w_ff2, g_final):
    depth = w_in.shape[0]
    batch, seq, d = x.shape
    q_scale = LOG2E * HEAD_DIM ** -0.5
    k_tables = _rope_tables(seq)
    q_tables = tuple(tb * q_scale for tb in k_tables)

    n_rows = -(-(batch + 1) // 8) * 8
    cond = jnp.zeros((n_rows, d), F32).at[:batch].set(c).at[batch].set(c_ctx)
    latent_row = lambda i: i
    ctx_row = lambda i: batch

    h_ctx = ctx
    for i in range(depth):
        ctx_out = i < depth - 1
        mod = adaln_modulation(cond, w_ada[i], b_ada[i][None]).reshape(n_rows, 1, N_MOD * d)

        wi = w_in[i]
        w_main = jnp.concatenate(
            [wi[:, REF_COL_Z:REF_COL_GATE], wi[:, REF_COL_GATE:], wi[:, REF_COL_Q:REF_COL_Z],
             wi[:, :REF_COL_DT]], axis=1).astype(BF16)
        w_dt_t = wi[:, REF_COL_DT:REF_COL_Q].reshape(d, 2, SSD_GROUPS, HEADS_PER_GROUP).transpose(
            2, 1, 3, 0).reshape(2 * SSD_HEADS, d).astype(BF16)
        ssd_params = (_group_major(dt_bias[i])[:, :, None], _group_major(a_log[i])[:, :, None])
        dskip = jnp.repeat(d_skip[i][0] + d_skip[i][1], SSD_HEAD_DIM)[None]
        sink = attn_sink[i]
        wa, ws, wo = w_o_attn[i].astype(BF16), w_o_ssd[i].astype(BF16), w_out[i].astype(BF16)
        w1, w2 = w_ff1[i].astype(BF16), w_ff2[i].astype(BF16)
        g1, g2, gs = g_norm1[i][None], g_norm2[i][None], g_ssd[i][None]

        ctx_lo = 0 if ctx_out else P_K
        hc = norm_modulate(h_ctx, g1, mod, ctx_row, 0)
        p_c, xbc_c, y_c, fin_c = _mixer_side(hc, ctx_lo, w_main, w_dt_t, conv_w[i],
                                             conv_b[i][None], ssd_params, None)
        hx = norm_modulate(x, g1, mod, latent_row, 0)
        p_x, xbc_x, y_x, _ = _mixer_side(hx, 0, w_main, w_dt_t, conv_w[i], conv_b[i][None],
                                         ssd_params, fin_c)
        att_x = latent_attention(p_x, p_c, -ctx_lo, sink, q_tables, k_tables)
        x = _merge_out(att_x, ssd_out_norm(y_x, xbc_x, p_x, dskip, gs), p_x, wa, ws, wo, x, mod,
                       latent_row)
        x = _mlp(x, g2, w1, w2, mod, latent_row)
        if ctx_out:
            att_c = context_attention(p_c, sink)
            h_ctx = _merge_out(att_c, ssd_out_norm(y_c, xbc_c, p_c, dskip, gs), p_c, wa, ws, wo,
                               h_ctx, mod, ctx_row)
            h_ctx = _mlp(h_ctx, g2, w1, w2, mod, ctx_row)
    return final_norm(x, g_final[None])
```

```python
import functools

import jax
import jax.numpy as jnp
from jax import lax
from jax.experimental import pallas as pl
from jax.experimental.pallas import tpu as pltpu

F32 = jnp.float32
BF16 = jnp.bfloat16

D_MODEL = 2048
GRID_W = 64
N_HEADS = 16
N_KV_HEADS = 4
HEAD_DIM = 128
REP = N_HEADS // N_KV_HEADS
ATT_WIDTH = N_HEADS * HEAD_DIM
KV_WIDTH = N_KV_HEADS * HEAD_DIM
WINDOW = 128
ATT_BLOCK = 128
ROPE_THETA = 10000.0
D_INNER = 2 * D_MODEL
SSD_HEAD_DIM = 64
SSD_HEADS = D_INNER // SSD_HEAD_DIM
SSD_GROUPS = 8
HEADS_PER_GROUP = SSD_HEADS // SSD_GROUPS
D_STATE = 128
CONV_K = 5
SSD_CHUNK = 128
GROUP_WIDTH = HEADS_PER_GROUP * SSD_HEAD_DIM
CONV_CH = D_INNER + 2 * SSD_GROUPS * D_STATE
D_FF = 4 * D_MODEL
N_MOD = 6
EPS = 1e-6
LOG2E = 1.4426950408889634
NEG_BIG = -1e30
LOG2_DT_FLOOR = -1e4

REF_COL_XBC = 2 * KV_WIDTH
REF_COL_DT = REF_COL_XBC + CONV_CH
REF_COL_Q = REF_COL_DT + 2 * SSD_HEADS
REF_COL_Z = REF_COL_Q + ATT_WIDTH
REF_COL_GATE = REF_COL_Z + D_INNER

P_Z = 0
P_GATE_A = P_Z + D_INNER
P_GATE_S = P_GATE_A + D_MODEL
P_Q = P_GATE_S + D_MODEL
P_K = P_Q + ATT_WIDTH
P_V = P_K + KV_WIDTH
P_XBC = P_V + KV_WIDTH
P_BM = P_XBC + D_INNER
P_CM = P_BM + SSD_GROUPS * D_STATE
P_COLS = P_CM + SSD_GROUPS * D_STATE

LANES = 128
VMEM_LIMIT = 56 * 1024 * 1024


def _params(*sem):
    return pltpu.CompilerParams(dimension_semantics=sem, vmem_limit_bytes=VMEM_LIMIT)


def _mm_kernel(*refs, n_pairs, n_extra, epilogue):
    a_refs = refs[:n_pairs]
    w_refs = refs[n_pairs:2 * n_pairs]
    extra = refs[2 * n_pairs:2 * n_pairs + n_extra]
    o_ref = refs[2 * n_pairs + n_extra]
    parts = [jnp.dot(a[...], w[...], preferred_element_type=F32) for a, w in zip(a_refs, w_refs)]
    o_ref[...] = epilogue(parts, [e[...] for e in extra]).astype(o_ref.dtype)


def matmul(pairs, extras, epilogue, n_out, out_dtype, tm, tn, w_col0=0, name="mm"):
    m = pairs[0][0].shape[0]
    tm = min(tm, m)
    assert m % tm == 0 and n_out % tn == 0 and w_col0 % tn == 0
    jo = w_col0 // tn
    in_specs = [pl.BlockSpec((tm, a.shape[1]), lambda i, j: (i, 0)) for a, _ in pairs]
    in_specs += [pl.BlockSpec((w.shape[0], tn), lambda i, j: (0, j + jo)) for _, w in pairs]
    in_specs += [pl.BlockSpec(bs, im) for _, bs, im in extras]
    kern = functools.partial(_mm_kernel, n_pairs=len(pairs), n_extra=len(extras), epilogue=epilogue)
    return pl.pallas_call(
        kern,
        grid=(m // tm, n_out // tn),
        in_specs=in_specs,
        out_specs=pl.BlockSpec((tm, tn), lambda i, j: (i, j)),
        out_shape=jax.ShapeDtypeStruct((m, n_out), out_dtype),
        compiler_params=_params("parallel", "arbitrary"),
        name=name,
    )(*[a for a, _ in pairs], *[w for _, w in pairs], *[e for e, _, _ in extras])


def _mm_acc_kernel(*refs, nk, n_extra, epilogue):
    a_ref, w_ref = refs[0], refs[1]
    extra = refs[2:2 + n_extra]
    o_ref, acc_ref = refs[2 + n_extra], refs[3 + n_extra]
    k = pl.program_id(2)
    part = jnp.dot(a_ref[...], w_ref[...], preferred_element_type=F32)

    @pl.when(k == 0)
    def _():
        acc_ref[...] = part

    @pl.when(k > 0)
    def _():
        acc_ref[...] += part

    @pl.when(k == nk - 1)
    def _():
        o_ref[...] = epilogue([acc_ref[...]], [e[...] for e in extra]).astype(o_ref.dtype)


def matmul_ktiled(a, w, extras, epilogue, out_dtype, tm, tn, tk, name="mmk"):
    m, kdim = a.shape
    n_out = w.shape[1]
    tm = min(tm, m)
    assert m % tm == 0 and n_out % tn == 0 and kdim % tk == 0
    nk = kdim // tk
    in_specs = [pl.BlockSpec((tm, tk), lambda i, j, k: (i, k)),
                pl.BlockSpec((tk, tn), lambda i, j, k: (k, j))]
    in_specs += [pl.BlockSpec(bs, (lambda i, j, k, im=im: im(i, j))) for _, bs, im in extras]
    kern = functools.partial(_mm_acc_kernel, nk=nk, n_extra=len(extras), epilogue=epilogue)
    return pl.pallas_call(
        kern,
        grid=(m // tm, n_out // tn, nk),
        in_specs=in_specs,
        out_specs=pl.BlockSpec((tm, tn), lambda i, j, k: (i, j)),
        out_shape=jax.ShapeDtypeStruct((m, n_out), out_dtype),
        scratch_shapes=[pltpu.VMEM((tm, tn), F32)],
        compiler_params=_params("parallel", "arbitrary", "arbitrary"),
        name=name,
    )(a, w, *[e for e, _, _ in extras])


def _adaln_kernel(c_ref, w_ref, b_ref, o_ref):
    c = c_ref[...]
    silu_c = c * jax.nn.sigmoid(c)
    o_ref[...] = jnp.dot(silu_c, w_ref[...], preferred_element_type=F32,
                         precision=lax.Precision.HIGHEST) + b_ref[...]


def adaln_modulation(cond, w, b, tn=512):
    r, d = cond.shape
    n = w.shape[1]
    return pl.pallas_call(
        _adaln_kernel,
        grid=(n // tn,),
        in_specs=[pl.BlockSpec((r, d), lambda j: (0, 0)),
                  pl.BlockSpec((d, tn), lambda j: (0, j)),
                  pl.BlockSpec((1, tn), lambda j: (0, j))],
        out_specs=pl.BlockSpec((r, tn), lambda j: (0, j)),
        out_shape=jax.ShapeDtypeStruct((r, n), F32),
        compiler_params=_params("arbitrary"),
        name="adaln",
    )(cond, w, b)


def _dt_proj_kernel(h_ref, wt_ref, ot_ref):
    ot_ref[...] = lax.dot_general(wt_ref[...], h_ref[...], (((1,), (1,)), ((), ())),
                                  preferred_element_type=F32)


def dt_projection(h, w_dt_t, tm=1024):
    m, d = h.shape
    tm = min(tm, m)
    return pl.pallas_call(
        _dt_proj_kernel,
        grid=(m // tm,),
        in_specs=[pl.BlockSpec((tm, d), lambda i: (i, 0)),
                  pl.BlockSpec((LANES, d), lambda i: (0, 0))],
        out_specs=pl.BlockSpec((LANES, tm), lambda i: (0, i)),
        out_shape=jax.ShapeDtypeStruct((LANES, m), F32),
        compiler_params=_params("parallel"),
        name="dt_proj",
    )(h, w_dt_t)


def _norm_mod_kernel(x_ref, g_ref, shift_ref, scale_ref, o_ref):
    x = x_ref[...]
    y = x * lax.rsqrt(jnp.mean(x * x, axis=-1, keepdims=True) + EPS) * g_ref[...]
    o_ref[...] = (y * (1.0 + scale_ref[...]) + shift_ref[...]).astype(o_ref.dtype)


def norm_modulate(x, g, mod, mod_row, shift_idx, ts=512):
    b, t, d = x.shape
    ts = min(ts, t)
    return pl.pallas_call(
        _norm_mod_kernel,
        grid=(b, t // ts),
        in_specs=[pl.BlockSpec((None, ts, d), lambda i, j: (i, j, 0)),
                  pl.BlockSpec((1, d), lambda i, j: (0, 0)),
                  pl.BlockSpec((None, 1, d), lambda i, j: (mod_row(i), 0, shift_idx)),
                  pl.BlockSpec((None, 1, d), lambda i, j: (mod_row(i), 0, shift_idx + 1))],
        out_specs=pl.BlockSpec((None, ts, d), lambda i, j: (i, j, 0)),
        out_shape=jax.ShapeDtypeStruct((b, t, d), BF16),
        compiler_params=_params("parallel", "parallel"),
        name="norm_mod",
    )(x, g, mod, mod)


def _final_norm_kernel(x_ref, g_ref, o_ref):
    x = x_ref[...]
    o_ref[...] = x * lax.rsqrt(jnp.mean(x * x, axis=-1, keepdims=True) + EPS) * g_ref[...]


def final_norm(x, g, ts=512):
    b, t, d = x.shape
    ts = min(ts, t)
    return pl.pallas_call(
        _final_norm_kernel,
        grid=(b, t // ts),
        in_specs=[pl.BlockSpec((None, ts, d), lambda i, j: (i, j, 0)),
                  pl.BlockSpec((1, d), lambda i, j: (0, 0))],
        out_specs=pl.BlockSpec((None, ts, d), lambda i, j: (i, j, 0)),
        out_shape=jax.ShapeDtypeStruct((b, t, d), F32),
        compiler_params=_params("parallel", "parallel"),
        name="final_norm",
    )(x, g)


CONV_ROWS = 128
CONV_HALO = 16
CONV_UNROLL = 8


def _conv_silu_kernel(u_ref, w_ref, b_ref, o_ref, ext_ref, *, unroll):
    t, tc = u_ref.shape
    pad = CONV_K // 2
    span = CONV_ROWS + 2 * CONV_HALO
    halo = jnp.zeros((CONV_HALO, tc), BF16)
    ext_ref[0:CONV_HALO, :] = halo
    ext_ref[CONV_HALO + t:, :] = halo
    ext_ref[CONV_HALO:CONV_HALO + t, :] = u_ref[...]
    taps = [k - pad for k in range(CONV_K) if k != pad]
    ri = lax.broadcasted_iota(jnp.int32, (len(taps) * CONV_ROWS, span), 0)
    ci = lax.broadcasted_iota(jnp.int32, (len(taps) * CONV_ROWS, span), 1)
    tap_off = jnp.zeros_like(ri)
    for n, off in enumerate(taps):
        tap_off = jnp.where(ri // CONV_ROWS == n, off, tap_off)
    shift_mat = (ci == (ri % CONV_ROWS) + CONV_HALO + tap_off).astype(BF16)
    bias = b_ref[...]
    w = [w_ref[k:k + 1, :] for k in range(CONV_K)]

    def body(it, carry):
        windows, shifted = [], []
        for n in range(unroll):
            r0 = pl.multiple_of((it * unroll + n) * CONV_ROWS, CONV_ROWS)
            win = ext_ref[pl.ds(r0, span), :]
            windows.append((r0, win))
            shifted.append(jnp.dot(shift_mat, win, preferred_element_type=F32))
        for (r0, win), sh in zip(windows, shifted):
            acc = bias + w[pad] * win[CONV_HALO:CONV_HALO + CONV_ROWS].astype(F32)
            for n, off in enumerate(taps):
                acc = acc + w[pad + off] * sh[n * CONV_ROWS:(n + 1) * CONV_ROWS]
            o_ref[pl.ds(r0, CONV_ROWS), :] = (acc * jax.nn.sigmoid(acc)).astype(o_ref.dtype)
        return carry

    lax.fori_loop(0, t // (CONV_ROWS * unroll), body, 0)


def conv_silu(p, col0, conv_w, conv_b, tc=512):
    b, t, _ = p.shape
    unroll = min(CONV_UNROLL, t // CONV_ROWS)
    assert t % (CONV_ROWS * unroll) == 0
    c0 = (col0 + P_XBC) // tc
    return pl.pallas_call(
        functools.partial(_conv_silu_kernel, unroll=unroll),
        grid=(b, CONV_CH // tc),
        in_specs=[pl.BlockSpec((None, t, tc), lambda i, j: (i, 0, c0 + j)),
                  pl.BlockSpec((CONV_K, tc), lambda i, j: (0, j)),
                  pl.BlockSpec((1, tc), lambda i, j: (0, j))],
        out_specs=pl.BlockSpec((None, t, tc), lambda i, j: (i, 0, j)),
        out_shape=jax.ShapeDtypeStruct((b, t, CONV_CH), BF16),
        scratch_shapes=[pltpu.VMEM((t + 2 * CONV_HALO, tc), BF16)],
        compiler_params=_params("parallel", "parallel"),
        name="conv_silu",
    )(p, conv_w, conv_b)


def _rope(u, cos, sin_next, sin_prev):
    quarter = HEAD_DIM // 4
    return (u * cos + pltpu.roll(u, HEAD_DIM - quarter, axis=1) * sin_next
            + pltpu.roll(u, quarter, axis=1) * sin_prev)


def _softmax_pv(s, sink_col, v):
    m = jnp.maximum(jnp.max(s, axis=-1, keepdims=True), sink_col)
    e = jnp.exp(s - m)
    denom = jnp.sum(e, axis=-1, keepdims=True) + jnp.exp(sink_col - m)
    o = jnp.dot(e.astype(BF16), v, preferred_element_type=F32)
    return o / denom


def _sink_column(sink_ref, h, rows):
    blk = lax.broadcasted_iota(jnp.int32, (REP * rows, 1), 0) // rows
    col = jnp.zeros((REP * rows, 1), F32)
    for r in range(REP):
        col = jnp.where(blk == r, sink_ref[h * REP + r], col)
    return col


def _latent_attn_kernel(sink_ref, q_ref, kp_ref, kc_ref, kn_ref, vp_ref, vc_ref, vn_ref,
                        kx_ref, vx_ref, qcos_ref, qsn_ref, qsp_ref, cos_ref, sn_ref, sp_ref, o_ref,
                        *, nb):
    n = pl.program_id(1)
    blk = ATT_BLOCK

    def rows(ref, i):
        return ref[pl.ds(pl.multiple_of(i * blk, blk), blk), :]

    n_prev = jnp.maximum(n - 1, 0)
    n_next = jnp.minimum(n + 1, nb - 1)
    k_tables = [[rows(ref, i) for ref in (cos_ref, sn_ref, sp_ref)] for i in (n_prev, n, n_next)]
    qcos, qsn, qsp = rows(qcos_ref, n), rows(qsn_ref, n), rows(qsp_ref, n)
    qi = lax.broadcasted_iota(jnp.int32, (blk, blk), 0)
    kj = lax.broadcasted_iota(jnp.int32, (blk, blk), 1)
    see_prev = kj >= qi + (n == 0).astype(jnp.int32) * blk
    see_next = kj <= qi - (n == nb - 1).astype(jnp.int32) * blk
    for h in range(N_KV_HEADS):
        kv = slice(h * HEAD_DIM, (h + 1) * HEAD_DIM)
        keys = jnp.concatenate(
            [_rope(ref[:, kv].astype(F32), *tabs).astype(BF16)
             for ref, tabs in zip((kp_ref, kc_ref, kn_ref), k_tables)] + [kx_ref[:, kv]], axis=0)
        vals = jnp.concatenate([vp_ref[:, kv], vc_ref[:, kv], vn_ref[:, kv], vx_ref[:, kv]], axis=0)
        heads = [slice((h * REP + r) * HEAD_DIM, (h * REP + r + 1) * HEAD_DIM) for r in range(REP)]
        logits = []
        for cols in heads:
            q = _rope(q_ref[:, cols].astype(F32), qcos, qsn, qsp).astype(BF16)
            logits.append(lax.dot_general(q, keys, (((1,), (1,)), ((), ())),
                                          preferred_element_type=F32))
        probs, denoms = [], []
        for r, s in enumerate(logits):
            s = jnp.concatenate(
                [jnp.where(see_prev, s[:, :blk], -jnp.inf), s[:, blk:2 * blk],
                 jnp.where(see_next, s[:, 2 * blk:3 * blk], -jnp.inf), s[:, 3 * blk:]], axis=1)
            sink = sink_ref[h * REP + r] * LOG2E
            m = jnp.maximum(jnp.max(s, axis=-1, keepdims=True), sink)
            e = jnp.exp2(s - m)
            denoms.append(jnp.sum(e, axis=-1, keepdims=True) + jnp.exp2(sink - m))
            probs.append(e.astype(BF16))
        for cols, e, denom in zip(heads, probs, denoms):
            o = jnp.dot(e, vals, preferred_element_type=F32)
            o_ref[:, cols] = (o / denom).astype(o_ref.dtype)


def latent_attention(p, p_ctx, ctx_col0, sink, q_tables, k_tables):
    b, s, _ = p.shape
    l = p_ctx.shape[1]
    nb = s // ATT_BLOCK
    blk = ATT_BLOCK
    kb, vb = P_K // KV_WIDTH, P_V // KV_WIDTH
    kxb, vxb = (ctx_col0 + P_K) // KV_WIDTH, (ctx_col0 + P_V) // KV_WIDTH
    prev = lambda n: jnp.maximum(n - 1, 0)
    nxt = lambda n: jnp.minimum(n + 1, nb - 1)
    kv_spec = lambda cb, f: pl.BlockSpec((None, blk, KV_WIDTH), lambda i, n: (i, f(n), cb))
    table = pl.BlockSpec((s, HEAD_DIM), lambda i, n: (0, 0))
    return pl.pallas_call(
        functools.partial(_latent_attn_kernel, nb=nb),
        grid=(b, nb),
        in_specs=[pl.BlockSpec(memory_space=pltpu.SMEM),
                  pl.BlockSpec((None, blk, ATT_WIDTH), lambda i, n: (i, n, P_Q // ATT_WIDTH)),
                  kv_spec(kb, prev), kv_spec(kb, lambda n: n), kv_spec(kb, nxt),
                  kv_spec(vb, prev), kv_spec(vb, lambda n: n), kv_spec(vb, nxt),
                  pl.BlockSpec((None, l, KV_WIDTH), lambda i, n: (i, 0, kxb)),
                  pl.BlockSpec((None, l, KV_WIDTH), lambda i, n: (i, 0, vxb)),
                  table, table, table, table, table, table],
        out_specs=pl.BlockSpec((None, blk, ATT_WIDTH), lambda i, n: (i, n, 0)),
        out_shape=jax.ShapeDtypeStruct((b, s, ATT_WIDTH), BF16),
        compiler_params=_params("parallel", "parallel"),
        name="latent_attn",
    )(sink, p, p, p, p, p, p, p, p_ctx, p_ctx, *q_tables, *k_tables)


def _context_attn_kernel(sink_ref, q_ref, k_ref, v_ref, o_ref):
    h = pl.program_id(2)
    blk = q_ref.shape[0]
    q = jnp.concatenate([q_ref[:, r * HEAD_DIM:(r + 1) * HEAD_DIM] for r in range(REP)], axis=0)
    s = lax.dot_general(q, k_ref[...], (((1,), (1,)), ((), ())), preferred_element_type=F32)
    o = _softmax_pv(s * HEAD_DIM ** -0.5, _sink_column(sink_ref, h, blk), v_ref[...])
    for r in range(REP):
        o_ref[:, r * HEAD_DIM:(r + 1) * HEAD_DIM] = o[r * blk:(r + 1) * blk].astype(o_ref.dtype)


def context_attention(p_ctx, sink):
    b, l, _ = p_ctx.shape
    blk = min(ATT_BLOCK, l)
    kb, vb = P_K // HEAD_DIM, P_V // HEAD_DIM
    qb = P_Q // (REP * HEAD_DIM)
    return pl.pallas_call(
        _context_attn_kernel,
        grid=(b, l // blk, N_KV_HEADS),
        in_specs=[pl.BlockSpec(memory_space=pltpu.SMEM),
                  pl.BlockSpec((None, blk, REP * HEAD_DIM), lambda i, n, h: (i, n, qb + h)),
                  pl.BlockSpec((None, l, HEAD_DIM), lambda i, n, h: (i, 0, kb + h)),
                  pl.BlockSpec((None, l, HEAD_DIM), lambda i, n, h: (i, 0, vb + h))],
        out_specs=pl.BlockSpec((None, blk, REP * HEAD_DIM), lambda i, n, h: (i, n, h)),
        out_shape=jax.ShapeDtypeStruct((b, l, ATT_WIDTH), BF16),
        compiler_params=_params("parallel", "parallel", "arbitrary"),
        name="context_attn",
    )(sink, p_ctx, p_ctx, p_ctx)


def _split3(x):
    hi = x.astype(BF16).astype(F32)
    r1 = x - hi
    mid = r1.astype(BF16).astype(F32)
    lo = (r1 - mid).astype(BF16).astype(F32)
    return hi, mid, lo


def _softplus(x):
    return jnp.maximum(x, 0.0) + jnp.log1p(jnp.exp(-jnp.abs(x)))


def _ssd_kernel(*refs, nc, zero_init, with_output):
    refs = list(refs)
    x_ref, b_ref, c_ref, dtt_ref, bias_ref, alog_ref = refs[:6]
    del refs[:6]
    z_ref, dskip_ref = (refs.pop(0), refs.pop(0)) if with_output else (None, None)
    init_ref = None if zero_init else refs.pop(0)
    u_ref, ssq_ref = (refs.pop(0), refs.pop(0)) if with_output else (None, None)
    fin_ref, state_ref, rhs_ref, decay_ref, exit_ref, entry_ref = refs[:6]
    y_ref = refs[6] if with_output else None
    q = SSD_CHUNK
    e_heads = HEADS_PER_GROUP
    wide = e_heads * q
    ii = lax.broadcasted_iota(jnp.int32, (q, q), 0)
    jj = lax.broadcasted_iota(jnp.int32, (q, q), 1)
    eye = (ii == jj).astype(BF16)
    left_b = (jj < SSD_HEAD_DIM).astype(BF16)
    right_b = (jj >= SSD_HEAD_DIM).astype(BF16)
    r32 = lax.broadcasted_iota(jnp.int32, (32, wide), 0)
    c32 = lax.broadcasted_iota(jnp.int32, (32, wide), 1) // q
    head_rows = ((r32 < 3 * e_heads) & ((r32 & (e_heads - 1)) == c32)).astype(F32)
    r8 = lax.broadcasted_iota(jnp.int32, (e_heads, wide), 0)
    c8 = lax.broadcasted_iota(jnp.int32, (e_heads, wide), 1) // q
    own_block = (r8 == c8).astype(F32)
    re = lax.broadcasted_iota(jnp.int32, (q, GROUP_WIDTH), 0)
    ce = lax.broadcasted_iota(jnp.int32, (q, GROUP_WIDTH), 1) // SSD_HEAD_DIM
    expand = ((re >= 64) & (re < 64 + 3 * e_heads) & ((re & (e_heads - 1)) == ce)).astype(BF16)
    zeros8 = jnp.zeros((e_heads, q), F32)
    ones24 = jnp.ones((3 * e_heads, q), F32)

    head_params = []
    for d in range(2):
        lo, hi = d * e_heads, (d + 1) * e_heads
        visible = (jj >= ii) if d else (jj <= ii)
        cum_mat = ((jj <= ii) if d else (jj >= ii)).astype(BF16)
        head_params.append((bias_ref[lo:hi, :], -jnp.exp(alog_ref[lo:hi, :]), cum_mat))
        rhs_ref[d, 0:q, :] = jnp.tile(jnp.where(visible, 0.0, NEG_BIG), (1, e_heads)).astype(BF16)
        rhs_ref[d, q:q + 32, :] = head_rows.astype(BF16)
        rhs_ref[d, q + 32:2 * q, :] = jnp.zeros((q - 32, wide), BF16)
        if zero_init:
            state_ref[d] = jnp.zeros(state_ref.shape[1:], F32)
        else:
            state_ref[d] = init_ref[d]

    def chunk_rows(d, step):
        c = (nc - 1 - step) if d else step
        return pl.ds(pl.multiple_of(c * q, q), q)

    def prepare_cumsum(d, step):
        lo, hi = d * e_heads, (d + 1) * e_heads
        bias_t, aneg_t, cum_mat = head_params[d]
        rows = chunk_rows(d, step)
        dt = _softplus(dtt_ref[lo:hi, rows] + bias_t)
        da = jnp.concatenate(_split3(dt * aneg_t), axis=0).astype(BF16)
        return dict(d=d, rows=rows, dt=dt, cs=jnp.dot(da, cum_mat, preferred_element_type=F32))

    def prepare_matmuls(v):
        d, dt, cs = v["d"], v["dt"], v["cs"]
        last = 0 if d else q - 1
        bc = b_ref[v["rows"], :]
        cc = c_ref[v["rows"], :]
        a2 = (cs[0:e_heads] + cs[e_heads:2 * e_heads] + cs[2 * e_heads:]) * LOG2E
        ea = jnp.exp2(a2)
        w_exit = jnp.exp2(a2[:, last:last + 1] - a2) * dt
        g = jnp.maximum(jnp.log2(dt), LOG2_DT_FLOOR) - a2
        rhs_ref[d, q + 32:q + 64, :] = jnp.concatenate(
            [jnp.tile(part, (1, e_heads)) * own_block for part in _split3(g)]
            + [jnp.zeros((e_heads, wide), F32)], axis=0).astype(BF16)
        split_t = jnp.concatenate(list(_split3(a2)) + [zeros8, ones24, zeros8]
                                  + list(_split3(ea)) + [jnp.zeros((q - 88, q), F32)], axis=0)
        split_n = split_t.T.astype(BF16)
        v["seg"] = jnp.dot(jnp.concatenate([eye, split_n], axis=1), rhs_ref[d],
                           preferred_element_type=F32)
        v["ea_wide"] = jnp.dot(split_n, expand, preferred_element_type=F32)
        v["cb"] = lax.dot_general(cc, bc, (((1,), (1,)), ((), ())), preferred_element_type=F32)
        b_t = bc.astype(F32).T
        v["b_exit"] = jnp.concatenate([b_t * w_exit[e:e + 1, :] for e in range(e_heads)],
                                      axis=1).astype(BF16)

    def prepare_store(v, slot):
        d = v["d"]
        decay_ref[slot, d] = (jnp.tile(v["cb"], (1, e_heads)) * jnp.exp2(v["seg"])).astype(BF16)
        exit_ref[slot, d] = v["b_exit"]
        entry_ref[slot, d] = v["ea_wide"]

    def emit_entry(d, step):
        rows = chunk_rows(d, step)
        state = state_ref[d]
        w = dict(d=d, rows=rows, state=state)
        if with_output:
            w["y_off"] = jnp.dot(c_ref[rows, :], state.astype(BF16), preferred_element_type=F32)
        return w

    def emit_outputs(w, slot, second_visit):
        d, rows = w["d"], w["rows"]
        last = 0 if d else q - 1
        xc = x_ref[rows, :]
        ssq = jnp.zeros((q, LANES), F32)
        for p in range(e_heads // 2):
            cols = slice(p * LANES, (p + 1) * LANES)
            two = slice(2 * p * q, 2 * (p + 1) * q)
            xp = xc[:, cols]
            x_pair = jnp.concatenate([xp * left_b, xp * right_b], axis=0)
            entry = entry_ref[slot, d, :, cols]
            if not with_output:
                new = jnp.dot(exit_ref[slot, d, :, two], x_pair, preferred_element_type=F32)
                state_ref[d, :, cols] = w["state"][:, cols] * entry[last:last + 1, :] + new
                continue
            lhs = jnp.concatenate([decay_ref[slot, d, :, two], exit_ref[slot, d, :, two]], axis=0)
            out = jnp.dot(lhs, x_pair, preferred_element_type=F32)
            state_ref[d, :, cols] = w["state"][:, cols] * entry[last:last + 1, :] + out[q:]
            y = out[:q] + w["y_off"][:, cols] * entry
            if not second_visit:
                y_ref[rows, cols] = y
                continue
            z = z_ref[rows, cols].astype(F32)
            y = y_ref[rows, cols] + y + xp.astype(F32) * dskip_ref[:, cols]
            gated = y * (z * jax.nn.sigmoid(z))
            u_ref[rows, cols] = gated.astype(u_ref.dtype)
            ssq = ssq + gated * gated
        if with_output and second_visit:
            ssq_ref[rows, :] = ssq

    def prepare_all(step, slot):
        chains = [prepare_cumsum(d, step) for d in range(2)]
        for v in chains:
            prepare_matmuls(v)
        for v in chains:
            prepare_store(v, slot)

    def body(second_visits):
        def run(it, carry):
            for half in range(2):
                step = 2 * it + half
                ahead = jnp.minimum(step + 1, nc - 1)
                chains = [prepare_cumsum(d, ahead) for d in range(2)]
                entries = [emit_entry(d, step) for d in range(2)]
                for v, w in zip(chains, entries):
                    emit_outputs(w, half, second_visits[half])
                    prepare_matmuls(v)
                for v in chains:
                    prepare_store(v, 1 - half)
            return carry
        return run

    prepare_all(0, 0)
    if nc == 2:
        lax.fori_loop(0, 1, body((False, True)), 0)
    else:
        lax.fori_loop(0, nc // 4, body((False, False)), 0)
        lax.fori_loop(nc // 4, nc // 2, body((True, True)), 0)
    for d in range(2):
        fin_ref[d] = state_ref[d]


def ssd_scan(xbc, dt_t, bias_t, alog_t, init, gate=None):
    b, t, _ = xbc.shape
    nc = t // SSD_CHUNK
    assert nc == 2 or nc % 4 == 0
    wide = HEADS_PER_GROUP * SSD_CHUNK
    g = SSD_GROUPS
    e2 = 2 * HEADS_PER_GROUP
    bm0 = D_INNER // D_STATE
    cm0 = bm0 + g
    state_shape = (2, D_STATE, GROUP_WIDTH)
    group_block = pl.BlockSpec((None, t, GROUP_WIDTH), lambda i, j: (i, 0, j))
    state_block = pl.BlockSpec((None, None) + state_shape, lambda i, j: (i, j, 0, 0, 0))
    in_specs = [group_block,
                pl.BlockSpec((None, t, D_STATE), lambda i, j: (i, 0, bm0 + j)),
                pl.BlockSpec((None, t, D_STATE), lambda i, j: (i, 0, cm0 + j)),
                pl.BlockSpec((e2, t), lambda i, j: (j, i)),
                pl.BlockSpec((None, e2, 1), lambda i, j: (j, 0, 0)),
                pl.BlockSpec((None, e2, 1), lambda i, j: (j, 0, 0))]
    args = [xbc, xbc, xbc, dt_t, bias_t, alog_t]
    out_specs, out_shape = [state_block], [jax.ShapeDtypeStruct((b, g) + state_shape, F32)]
    scratch = [pltpu.VMEM(state_shape, F32),
               pltpu.VMEM((2, 2 * SSD_CHUNK, wide), BF16),
               pltpu.VMEM((2, 2, SSD_CHUNK, wide), BF16),
               pltpu.VMEM((2, 2, D_STATE, wide), BF16),
               pltpu.VMEM((2, 2, SSD_CHUNK, GROUP_WIDTH), F32)]
    if gate is not None:
        p, dskip = gate
        z0 = P_Z // GROUP_WIDTH
        in_specs += [pl.BlockSpec((None, t, GROUP_WIDTH), lambda i, j: (i, 0, z0 + j)),
                     pl.BlockSpec((None, 1, GROUP_WIDTH), lambda i, j: (j, 0, 0))]
        args += [p, dskip]
        out_specs = [group_block, pl.BlockSpec((None, t, LANES), lambda i, j: (i, 0, j))] + out_specs
        out_shape = [jax.ShapeDtypeStruct((b, t, D_INNER), BF16),
                     jax.ShapeDtypeStruct((b, t, g * LANES), F32)] + out_shape
        scratch.append(pltpu.VMEM((t, GROUP_WIDTH), F32))
    if init is not None:
        in_specs.append(state_block)
        args.append(init)
    outs = pl.pallas_call(
        functools.partial(_ssd_kernel, nc=nc, zero_init=init is None, with_output=gate is not None),
        grid=(b, g),
        in_specs=in_specs,
        out_specs=out_specs,
        out_shape=out_shape,
        scratch_shapes=scratch,
        compiler_params=_params("parallel", "parallel"),
        name="ssd_scan",
    )(*args)
    return tuple(outs) if gate is not None else (None, None, outs[0])


def _rope_tables(seq):
    t = jnp.arange(seq)
    row = (t // GRID_W).astype(F32)
    col = (t % GRID_W).astype(F32)
    axis_dim = HEAD_DIM // 2
    inv_freq = ROPE_THETA ** (-jnp.arange(0, axis_dim, 2, dtype=F32) / axis_dim)
    ang_r = row[:, None] * inv_freq[None]
    ang_c = col[:, None] * inv_freq[None]
    cos_r, sin_r, cos_c, sin_c = jnp.cos(ang_r), jnp.sin(ang_r), jnp.cos(ang_c), jnp.sin(ang_c)
    zero = jnp.zeros_like(sin_r)
    cos = jnp.concatenate([cos_r, cos_r, cos_c, cos_c], axis=-1)
    sin_next = jnp.concatenate([-sin_r, zero, -sin_c, zero], axis=-1)
    sin_prev = jnp.concatenate([zero, sin_r, zero, sin_c], axis=-1)
    return cos, sin_next, sin_prev


def _group_major(v):
    return v.reshape(2, SSD_GROUPS, HEADS_PER_GROUP).transpose(1, 0, 2).reshape(
        SSD_GROUPS, 2 * HEADS_PER_GROUP)


def _mixer_side(h, p_cols_lo, w_main, w_dt_t, conv_w, conv_b, ssd_params, init, dskip):
    b, t, d = h.shape
    h2 = h.reshape(b * t, d)
    n_out = P_COLS - p_cols_lo
    p = matmul([(h2, w_main)], [], lambda parts, ex: parts[0], n_out, BF16, tm=1024, tn=1024,
               w_col0=p_cols_lo, name="in_proj").reshape(b, t, n_out)
    xbc = conv_silu(p, -p_cols_lo, conv_w, conv_b)
    gate = None if dskip is None else (p, dskip)
    u, ssq, fin = ssd_scan(xbc, dt_projection(h2, w_dt_t), *ssd_params, init, gate)
    return p, u, ssq, fin


def _merge_out(att, u, ssq, p, w_o_attn, w_o_ssd_g, w_out, x, mod, mod_row):
    b, t, d = x.shape
    m = b * t
    tm, tn = min(512, t), 1024
    p2 = p.reshape(m, p.shape[-1])
    ga0, gs0 = P_GATE_A // tn, P_GATE_S // tn
    n_part = ssq.shape[-1]

    def merge(parts, ex):
        ga, gs, sq = ex
        inv_rms = lax.rsqrt(jnp.sum(sq, axis=-1, keepdims=True) * (1.0 / D_INNER) + EPS)
        return (jax.nn.sigmoid(ga.astype(F32)) * parts[0]
                + jax.nn.sigmoid(gs.astype(F32)) * (inv_rms * parts[1]))

    merged = matmul([(att.reshape(m, ATT_WIDTH), w_o_attn), (u.reshape(m, D_INNER), w_o_ssd_g)],
                    [(p2, (tm, tn), lambda i, j: (i, ga0 + j)),
                     (p2, (tm, tn), lambda i, j: (i, gs0 + j)),
                     (ssq.reshape(m, n_part), (tm, n_part), lambda i, j: (i, 0))],
                    merge, d, BF16, tm=tm, tn=tn, name="merge")
    return _residual_matmul(merged, w_out, x, mod, mod_row, 2, name="out_proj")


def _residual_matmul(a, w, x, mod, mod_row, gate_idx, name):
    b, t, d = x.shape
    ktiled = a.shape[1] > 4096
    tm, tn = (min(1024, t), 1024) if ktiled else (min(512, t), d)
    per_seq = t // tm
    nj = d // tn
    extras = [(x.reshape(b * t, d), (tm, tn), lambda i, j: (i, j)),
              (mod, (None, 1, tn), lambda i, j: (mod_row(i // per_seq), 0, gate_idx * nj + j))]
    res = lambda parts, ex: ex[0] + ex[1] * parts[0]
    if ktiled:
        out = matmul_ktiled(a, w, extras, res, F32, tm=tm, tn=tn, tk=2048, name=name)
    else:
        out = matmul([(a, w)], extras, res, d, F32, tm=tm, tn=tn, name=name)
    return out.reshape(b, t, d)


def _mlp(x, g, w1, w2, mod, mod_row):
    b, t, d = x.shape
    h = norm_modulate(x, g, mod, mod_row, 3).reshape(b * t, d)
    hid = matmul([(h, w1)], [], lambda parts, ex: jnp.square(jnp.maximum(parts[0], 0.0)), D_FF,
                 BF16, tm=1024, tn=1024, name="ff1")
    return _residual_matmul(hid, w2, x, mod, mod_row, 5, name="ff2")


def kernel(x, c, ctx, c_ctx, w_ada, b_ada, g_norm1, g_norm2, w_in, attn_sink, conv_w, conv_b,
           dt_bias, a_log, d_skip, g_ssd, w_o_attn, w_o_ssd, w_out, w_ff1, w_ff2, g_final):
    depth = w_in.shape[0]
    batch, seq, d = x.shape
    q_scale = LOG2E * HEAD_DIM ** -0.5
    k_tables = _rope_tables(seq)
    q_tables = tuple(tb * q_scale for tb in k_tables)

    n_rows = -(-(batch + 1) // 8) * 8
    cond = jnp.zeros((n_rows, d), F32).at[:batch].set(c).at[batch].set(c_ctx)
    latent_row = lambda i: i
    ctx_row = lambda i: batch

    h_ctx = ctx
    for i in range(depth):
        ctx_out = i < depth - 1
        mod = adaln_modulation(cond, w_ada[i], b_ada[i][None]).reshape(n_rows, 1, N_MOD * d)

        wi = w_in[i]
        w_main = jnp.concatenate(
            [wi[:, REF_COL_Z:REF_COL_GATE], wi[:, REF_COL_GATE:], wi[:, REF_COL_Q:REF_COL_Z],
             wi[:, :REF_COL_DT]], axis=1).astype(BF16)
        w_dt_t = wi[:, REF_COL_DT:REF_COL_Q].reshape(d, 2, SSD_GROUPS, HEADS_PER_GROUP).transpose(
            2, 1, 3, 0).reshape(2 * SSD_HEADS, d).astype(BF16)
        ssd_params = (_group_major(dt_bias[i])[:, :, None], _group_major(a_log[i])[:, :, None])
        dskip = jnp.repeat(d_skip[i][0] + d_skip[i][1], SSD_HEAD_DIM).reshape(
            SSD_GROUPS, 1, GROUP_WIDTH)
        sink = attn_sink[i]
        wa, wo = w_o_attn[i].astype(BF16), w_out[i].astype(BF16)
        ws = (g_ssd[i][:, None] * w_o_ssd[i]).astype(BF16)
        w1, w2 = w_ff1[i].astype(BF16), w_ff2[i].astype(BF16)
        g1, g2 = g_norm1[i][None], g_norm2[i][None]

        ctx_lo = 0 if ctx_out else P_K
        hc = norm_modulate(h_ctx, g1, mod, ctx_row, 0)
        p_c, u_c, ssq_c, fin_c = _mixer_side(hc, ctx_lo, w_main, w_dt_t, conv_w[i],
                                             conv_b[i][None], ssd_params, None,
                                             dskip if ctx_out else None)
        hx = norm_modulate(x, g1, mod, latent_row, 0)
        p_x, u_x, ssq_x, _ = _mixer_side(hx, 0, w_main, w_dt_t, conv_w[i], conv_b[i][None],
                                         ssd_params, fin_c, dskip)
        att_x = latent_attention(p_x, p_c, -ctx_lo, sink, q_tables, k_tables)
        x = _merge_out(att_x, u_x, ssq_x, p_x, wa, ws, wo, x, mod, latent_row)
        x = _mlp(x, g2, w1, w2, mod, latent_row)
        if ctx_out:
            att_c = context_attention(p_c, sink)
            h_ctx = _merge_out(att_c, u_c, ssq_c, p_c, wa, ws, wo, h_ctx, mod, ctx_row)
            h_ctx = _mlp(h_ctx, g2, w1, w2, mod, ctx_row)
    return final_norm(x, g_final[None])
```

```python
import functools

import jax
import jax.numpy as jnp
from jax import lax
from jax.experimental import pallas as pl
from jax.experimental.pallas import tpu as pltpu

F32 = jnp.float32
BF16 = jnp.bfloat16

D_MODEL = 2048
GRID_W = 64
N_HEADS = 16
N_KV_HEADS = 4
HEAD_DIM = 128
REP = N_HEADS // N_KV_HEADS
ATT_WIDTH = N_HEADS * HEAD_DIM
KV_WIDTH = N_KV_HEADS * HEAD_DIM
WINDOW = 128
ATT_BLOCK = 128
ROPE_THETA = 10000.0
D_INNER = 2 * D_MODEL
SSD_HEAD_DIM = 64
SSD_HEADS = D_INNER // SSD_HEAD_DIM
SSD_GROUPS = 8
HEADS_PER_GROUP = SSD_HEADS // SSD_GROUPS
D_STATE = 128
CONV_K = 5
SSD_CHUNK = 128
GROUP_WIDTH = HEADS_PER_GROUP * SSD_HEAD_DIM
CONV_CH = D_INNER + 2 * SSD_GROUPS * D_STATE
D_FF = 4 * D_MODEL
N_MOD = 6
EPS = 1e-6
LOG2E = 1.4426950408889634
NEG_BIG = -1e30
LOG2_DT_FLOOR = -1e4

REF_COL_XBC = 2 * KV_WIDTH
REF_COL_DT = REF_COL_XBC + CONV_CH
REF_COL_Q = REF_COL_DT + 2 * SSD_HEADS
REF_COL_Z = REF_COL_Q + ATT_WIDTH
REF_COL_GATE = REF_COL_Z + D_INNER

P_Z = 0
P_GATE_A = P_Z + D_INNER
P_GATE_S = P_GATE_A + D_MODEL
P_Q = P_GATE_S + D_MODEL
P_K = P_Q + ATT_WIDTH
P_V = P_K + KV_WIDTH
P_XBC = P_V + KV_WIDTH
P_BM = P_XBC + D_INNER
P_CM = P_BM + SSD_GROUPS * D_STATE
P_COLS = P_CM + SSD_GROUPS * D_STATE

LANES = 128
VMEM_LIMIT = 56 * 1024 * 1024


def _params(*sem):
    return pltpu.CompilerParams(dimension_semantics=sem, vmem_limit_bytes=VMEM_LIMIT)


def _mm_kernel(*refs, n_pairs, n_extra, epilogue):
    a_refs = refs[:n_pairs]
    w_refs = refs[n_pairs:2 * n_pairs]
    extra = refs[2 * n_pairs:2 * n_pairs + n_extra]
    o_ref = refs[2 * n_pairs + n_extra]
    parts = [jnp.dot(a[...], w[...], preferred_element_type=F32) for a, w in zip(a_refs, w_refs)]
    o_ref[...] = epilogue(parts, [e[...] for e in extra]).astype(o_ref.dtype)


def matmul(pairs, extras, epilogue, n_out, out_dtype, tm, tn, w_col0=0, w_stationary=False,
           name="mm"):
    m = pairs[0][0].shape[0]
    tm = min(tm, m)
    assert m % tm == 0 and n_out % tn == 0 and w_col0 % tn == 0
    jo = w_col0 // tn
    if w_stationary:
        grid = (n_out // tn, m // tm)
        at = lambda im: (lambda j, i: im(i, j))
    else:
        grid = (m // tm, n_out // tn)
        at = lambda im: im
    in_specs = [pl.BlockSpec((tm, a.shape[1]), at(lambda i, j: (i, 0))) for a, _ in pairs]
    in_specs += [pl.BlockSpec((w.shape[0], tn), at(lambda i, j: (0, j + jo))) for _, w in pairs]
    in_specs += [pl.BlockSpec(bs, at(im)) for _, bs, im in extras]
    kern = functools.partial(_mm_kernel, n_pairs=len(pairs), n_extra=len(extras), epilogue=epilogue)
    return pl.pallas_call(
        kern,
        grid=grid,
        in_specs=in_specs,
        out_specs=pl.BlockSpec((tm, tn), at(lambda i, j: (i, j))),
        out_shape=jax.ShapeDtypeStruct((m, n_out), out_dtype),
        compiler_params=_params("parallel", "arbitrary"),
        name=name,
    )(*[a for a, _ in pairs], *[w for _, w in pairs], *[e for e, _, _ in extras])


def _mm_acc_kernel(*refs, nk, n_extra, epilogue):
    a_ref, w_ref = refs[0], refs[1]
    extra = refs[2:2 + n_extra]
    o_ref, acc_ref = refs[2 + n_extra], refs[3 + n_extra]
    k = pl.program_id(2)
    part = jnp.dot(a_ref[...], w_ref[...], preferred_element_type=F32)

    @pl.when(k == 0)
    def _():
        acc_ref[...] = part

    @pl.when(k > 0)
    def _():
        acc_ref[...] += part

    @pl.when(k == nk - 1)
    def _():
        o_ref[...] = epilogue([acc_ref[...]], [e[...] for e in extra]).astype(o_ref.dtype)


def matmul_ktiled(a, w, extras, epilogue, out_dtype, tm, tn, tk, name="mmk"):
    m, kdim = a.shape
    n_out = w.shape[1]
    tm = min(tm, m)
    assert m % tm == 0 and n_out % tn == 0 and kdim % tk == 0
    nk = kdim // tk
    in_specs = [pl.BlockSpec((tm, tk), lambda i, j, k: (i, k)),
                pl.BlockSpec((tk, tn), lambda i, j, k: (k, j))]
    in_specs += [pl.BlockSpec(bs, (lambda i, j, k, im=im: im(i, j))) for _, bs, im in extras]
    kern = functools.partial(_mm_acc_kernel, nk=nk, n_extra=len(extras), epilogue=epilogue)
    return pl.pallas_call(
        kern,
        grid=(m // tm, n_out // tn, nk),
        in_specs=in_specs,
        out_specs=pl.BlockSpec((tm, tn), lambda i, j, k: (i, j)),
        out_shape=jax.ShapeDtypeStruct((m, n_out), out_dtype),
        scratch_shapes=[pltpu.VMEM((tm, tn), F32)],
        compiler_params=_params("parallel", "arbitrary", "arbitrary"),
        name=name,
    )(a, w, *[e for e, _, _ in extras])


def _adaln_kernel(c_ref, w_ref, b_ref, o_ref):
    c = c_ref[...]
    silu_c = c * jax.nn.sigmoid(c)
    o_ref[...] = jnp.dot(silu_c, w_ref[...], preferred_element_type=F32,
                         precision=lax.Precision.HIGHEST) + b_ref[...]


def adaln_modulation(cond, w, b, tn=512):
    r, d = cond.shape
    n = w.shape[1]
    return pl.pallas_call(
        _adaln_kernel,
        grid=(n // tn,),
        in_specs=[pl.BlockSpec((r, d), lambda j: (0, 0)),
                  pl.BlockSpec((d, tn), lambda j: (0, j)),
                  pl.BlockSpec((1, tn), lambda j: (0, j))],
        out_specs=pl.BlockSpec((r, tn), lambda j: (0, j)),
        out_shape=jax.ShapeDtypeStruct((r, n), F32),
        compiler_params=_params("arbitrary"),
        name="adaln",
    )(cond, w, b)


def _norm_mod_kernel(x_ref, g_ref, shift_ref, scale_ref, *rest):
    x = x_ref[...]
    y = x * lax.rsqrt(jnp.mean(x * x, axis=-1, keepdims=True) + EPS) * g_ref[...]
    h = (y * (1.0 + scale_ref[...]) + shift_ref[...]).astype(BF16)
    if len(rest) == 1:
        rest[0][...] = h
        return
    wt_ref, o_ref, dt_ref = rest
    o_ref[...] = h
    dt_ref[...] = lax.dot_general(wt_ref[...], h, (((1,), (1,)), ((), ())),
                                  preferred_element_type=F32)


def norm_modulate(x, g, mod, mod_row, shift_idx, w_dt_t=None, ts=512):
    b, t, d = x.shape
    ts = min(ts, t)
    per_seq = t // ts
    in_specs = [pl.BlockSpec((None, ts, d), lambda i, j: (i, j, 0)),
                pl.BlockSpec((1, d), lambda i, j: (0, 0)),
                pl.BlockSpec((None, 1, d), lambda i, j: (mod_row(i), 0, shift_idx)),
                pl.BlockSpec((None, 1, d), lambda i, j: (mod_row(i), 0, shift_idx + 1))]
    out_specs = pl.BlockSpec((None, ts, d), lambda i, j: (i, j, 0))
    out_shape = jax.ShapeDtypeStruct((b, t, d), BF16)
    args = [x, g, mod, mod]
    if w_dt_t is not None:
        in_specs.append(pl.BlockSpec((LANES, d), lambda i, j: (0, 0)))
        out_specs = [out_specs, pl.BlockSpec((LANES, ts), lambda i, j: (0, i * per_seq + j))]
        out_shape = [out_shape, jax.ShapeDtypeStruct((LANES, b * t), F32)]
        args.append(w_dt_t)
    return pl.pallas_call(
        _norm_mod_kernel,
        grid=(b, t // ts),
        in_specs=in_specs,
        out_specs=out_specs,
        out_shape=out_shape,
        compiler_params=_params("parallel", "parallel"),
        name="norm_mod",
    )(*args)


def _final_norm_kernel(x_ref, g_ref, o_ref):
    x = x_ref[...]
    o_ref[...] = x * lax.rsqrt(jnp.mean(x * x, axis=-1, keepdims=True) + EPS) * g_ref[...]


def final_norm(x, g, ts=512):
    b, t, d = x.shape
    ts = min(ts, t)
    return pl.pallas_call(
        _final_norm_kernel,
        grid=(b, t // ts),
        in_specs=[pl.BlockSpec((None, ts, d), lambda i, j: (i, j, 0)),
                  pl.BlockSpec((1, d), lambda i, j: (0, 0))],
        out_specs=pl.BlockSpec((None, ts, d), lambda i, j: (i, j, 0)),
        out_shape=jax.ShapeDtypeStruct((b, t, d), F32),
        compiler_params=_params("parallel", "parallel"),
        name="final_norm",
    )(x, g)


CONV_ROWS = 128
CONV_HALO = 16
CONV_UNROLL = 8


def _conv_silu_kernel(u_ref, w_ref, b_ref, o_ref, ext_ref, *, unroll):
    t, tc = u_ref.shape
    pad = CONV_K // 2
    span = CONV_ROWS + 2 * CONV_HALO
    halo = jnp.zeros((CONV_HALO, tc), BF16)
    ext_ref[0:CONV_HALO, :] = halo
    ext_ref[CONV_HALO + t:, :] = halo
    ext_ref[CONV_HALO:CONV_HALO + t, :] = u_ref[...]
    taps = [k - pad for k in range(CONV_K) if k != pad]
    ri = lax.broadcasted_iota(jnp.int32, (len(taps) * CONV_ROWS, span), 0)
    ci = lax.broadcasted_iota(jnp.int32, (len(taps) * CONV_ROWS, span), 1)
    tap_off = jnp.zeros_like(ri)
    for n, off in enumerate(taps):
        tap_off = jnp.where(ri // CONV_ROWS == n, off, tap_off)
    shift_mat = (ci == (ri % CONV_ROWS) + CONV_HALO + tap_off).astype(BF16)
    bias = b_ref[...]
    w = [w_ref[k:k + 1, :] for k in range(CONV_K)]

    def body(it, carry):
        windows, shifted = [], []
        for n in range(unroll):
            r0 = pl.multiple_of((it * unroll + n) * CONV_ROWS, CONV_ROWS)
            win = ext_ref[pl.ds(r0, span), :]
            windows.append((r0, win))
            shifted.append(jnp.dot(shift_mat, win, preferred_element_type=F32))
        for (r0, win), sh in zip(windows, shifted):
            acc = bias + w[pad] * win[CONV_HALO:CONV_HALO + CONV_ROWS].astype(F32)
            for n, off in enumerate(taps):
                acc = acc + w[pad + off] * sh[n * CONV_ROWS:(n + 1) * CONV_ROWS]
            o_ref[pl.ds(r0, CONV_ROWS), :] = (acc * jax.nn.sigmoid(acc)).astype(o_ref.dtype)
        return carry

    lax.fori_loop(0, t // (CONV_ROWS * unroll), body, 0)


def conv_silu(p, col0, conv_w, conv_b):
    b, t, _ = p.shape
    tc = 512 if t >= 1024 else 1024
    unroll = min(CONV_UNROLL, t // CONV_ROWS)
    assert t % (CONV_ROWS * unroll) == 0
    c0 = (col0 + P_XBC) // tc
    return pl.pallas_call(
        functools.partial(_conv_silu_kernel, unroll=unroll),
        grid=(b, CONV_CH // tc),
        in_specs=[pl.BlockSpec((None, t, tc), lambda i, j: (i, 0, c0 + j)),
                  pl.BlockSpec((CONV_K, tc), lambda i, j: (0, j)),
                  pl.BlockSpec((1, tc), lambda i, j: (0, j))],
        out_specs=pl.BlockSpec((None, t, tc), lambda i, j: (i, 0, j)),
        out_shape=jax.ShapeDtypeStruct((b, t, CONV_CH), BF16),
        scratch_shapes=[pltpu.VMEM((t + 2 * CONV_HALO, tc), BF16)],
        compiler_params=_params("parallel", "parallel"),
        name="conv_silu",
    )(p, conv_w, conv_b)


def _rope(u, cos, sin_next, sin_prev):
    quarter = HEAD_DIM // 4
    return (u * cos + pltpu.roll(u, HEAD_DIM - quarter, axis=1) * sin_next
            + pltpu.roll(u, quarter, axis=1) * sin_prev)


def _softmax_pv(s, sink_col, v):
    m = jnp.maximum(jnp.max(s, axis=-1, keepdims=True), sink_col)
    e = jnp.exp(s - m)
    denom = jnp.sum(e, axis=-1, keepdims=True) + jnp.exp(sink_col - m)
    o = jnp.dot(e.astype(BF16), v, preferred_element_type=F32)
    return o / denom


def _sink_column(sink_ref, h, rows):
    blk = lax.broadcasted_iota(jnp.int32, (REP * rows, 1), 0) // rows
    col = jnp.zeros((REP * rows, 1), F32)
    for r in range(REP):
        col = jnp.where(blk == r, sink_ref[h * REP + r], col)
    return col


def _latent_attn_kernel(sink_ref, q_ref, kp_ref, kc_ref, kn_ref, vp_ref, vc_ref, vn_ref,
                        kx_ref, vx_ref, qcos_ref, qsn_ref, qsp_ref, cos_ref, sn_ref, sp_ref, o_ref,
                        *, nb):
    n = pl.program_id(1)
    blk = ATT_BLOCK

    def rows(ref, i):
        return ref[pl.ds(pl.multiple_of(i * blk, blk), blk), :]

    n_prev = jnp.maximum(n - 1, 0)
    n_next = jnp.minimum(n + 1, nb - 1)
    k_tables = [[rows(ref, i) for ref in (cos_ref, sn_ref, sp_ref)] for i in (n_prev, n, n_next)]
    qcos, qsn, qsp = rows(qcos_ref, n), rows(qsn_ref, n), rows(qsp_ref, n)
    qi = lax.broadcasted_iota(jnp.int32, (blk, blk), 0)
    kj = lax.broadcasted_iota(jnp.int32, (blk, blk), 1)
    see_prev = kj >= qi + (n == 0).astype(jnp.int32) * blk
    see_next = kj <= qi - (n == nb - 1).astype(jnp.int32) * blk
    for h in range(N_KV_HEADS):
        kv = slice(h * HEAD_DIM, (h + 1) * HEAD_DIM)
        keys = jnp.concatenate(
            [_rope(ref[:, kv].astype(F32), *tabs).astype(BF16)
             for ref, tabs in zip((kp_ref, kc_ref, kn_ref), k_tables)] + [kx_ref[:, kv]], axis=0)
        vals = jnp.concatenate([vp_ref[:, kv], vc_ref[:, kv], vn_ref[:, kv], vx_ref[:, kv]], axis=0)
        heads = [slice((h * REP + r) * HEAD_DIM, (h * REP + r + 1) * HEAD_DIM) for r in range(REP)]
        logits = []
        for cols in heads:
            q = _rope(q_ref[:, cols].astype(F32), qcos, qsn, qsp).astype(BF16)
            logits.append(lax.dot_general(q, keys, (((1,), (1,)), ((), ())),
                                          preferred_element_type=F32))
        probs, denoms = [], []
        for r, s in enumerate(logits):
            s = jnp.concatenate(
                [jnp.where(see_prev, s[:, :blk], -jnp.inf), s[:, blk:2 * blk],
                 jnp.where(see_next, s[:, 2 * blk:3 * blk], -jnp.inf), s[:, 3 * blk:]], axis=1)
            sink = sink_ref[h * REP + r] * LOG2E
            m = jnp.maximum(jnp.max(s, axis=-1, keepdims=True), sink)
            e = jnp.exp2(s - m)
            denoms.append(jnp.sum(e, axis=-1, keepdims=True) + jnp.exp2(sink - m))
            probs.append(e.astype(BF16))
        for cols, e, denom in zip(heads, probs, denoms):
            o = jnp.dot(e, vals, preferred_element_type=F32)
            o_ref[:, cols] = (o / denom).astype(o_ref.dtype)


def latent_attention(p, p_ctx, ctx_col0, sink, q_tables, k_tables):
    b, s, _ = p.shape
    l = p_ctx.shape[1]
    nb = s // ATT_BLOCK
    blk = ATT_BLOCK
    kb, vb = P_K // KV_WIDTH, P_V // KV_WIDTH
    kxb, vxb = (ctx_col0 + P_K) // KV_WIDTH, (ctx_col0 + P_V) // KV_WIDTH
    prev = lambda n: jnp.maximum(n - 1, 0)
    nxt = lambda n: jnp.minimum(n + 1, nb - 1)
    kv_spec = lambda cb, f: pl.BlockSpec((None, blk, KV_WIDTH), lambda i, n: (i, f(n), cb))
    table = pl.BlockSpec((s, HEAD_DIM), lambda i, n: (0, 0))
    return pl.pallas_call(
        functools.partial(_latent_attn_kernel, nb=nb),
        grid=(b, nb),
        in_specs=[pl.BlockSpec(memory_space=pltpu.SMEM),
                  pl.BlockSpec((None, blk, ATT_WIDTH), lambda i, n: (i, n, P_Q // ATT_WIDTH)),
                  kv_spec(kb, prev), kv_spec(kb, lambda n: n), kv_spec(kb, nxt),
                  kv_spec(vb, prev), kv_spec(vb, lambda n: n), kv_spec(vb, nxt),
                  pl.BlockSpec((None, l, KV_WIDTH), lambda i, n: (i, 0, kxb)),
                  pl.BlockSpec((None, l, KV_WIDTH), lambda i, n: (i, 0, vxb)),
                  table, table, table, table, table, table],
        out_specs=pl.BlockSpec((None, blk, ATT_WIDTH), lambda i, n: (i, n, 0)),
        out_shape=jax.ShapeDtypeStruct((b, s, ATT_WIDTH), BF16),
        compiler_params=_params("parallel", "parallel"),
        name="latent_attn",
    )(sink, p, p, p, p, p, p, p, p_ctx, p_ctx, *q_tables, *k_tables)


def _context_attn_kernel(sink_ref, q_ref, k_ref, v_ref, o_ref):
    h = pl.program_id(2)
    blk = q_ref.shape[0]
    q = jnp.concatenate([q_ref[:, r * HEAD_DIM:(r + 1) * HEAD_DIM] for r in range(REP)], axis=0)
    s = lax.dot_general(q, k_ref[...], (((1,), (1,)), ((), ())), preferred_element_type=F32)
    o = _softmax_pv(s * HEAD_DIM ** -0.5, _sink_column(sink_ref, h, blk), v_ref[...])
    for r in range(REP):
        o_ref[:, r * HEAD_DIM:(r + 1) * HEAD_DIM] = o[r * blk:(r + 1) * blk].astype(o_ref.dtype)


def context_attention(p_ctx, sink):
    b, l, _ = p_ctx.shape
    blk = min(ATT_BLOCK, l)
    kb, vb = P_K // HEAD_DIM, P_V // HEAD_DIM
    qb = P_Q // (REP * HEAD_DIM)
    return pl.pallas_call(
        _context_attn_kernel,
        grid=(b, l // blk, N_KV_HEADS),
        in_specs=[pl.BlockSpec(memory_space=pltpu.SMEM),
                  pl.BlockSpec((None, blk, REP * HEAD_DIM), lambda i, n, h: (i, n, qb + h)),
                  pl.BlockSpec((None, l, HEAD_DIM), lambda i, n, h: (i, 0, kb + h)),
                  pl.BlockSpec((None, l, HEAD_DIM), lambda i, n, h: (i, 0, vb + h))],
        out_specs=pl.BlockSpec((None, blk, REP * HEAD_DIM), lambda i, n, h: (i, n, h)),
        out_shape=jax.ShapeDtypeStruct((b, l, ATT_WIDTH), BF16),
        compiler_params=_params("parallel", "parallel", "arbitrary"),
        name="context_attn",
    )(sink, p_ctx, p_ctx, p_ctx)


def _split3(x):
    hi = x.astype(BF16).astype(F32)
    r1 = x - hi
    mid = r1.astype(BF16).astype(F32)
    lo = (r1 - mid).astype(BF16).astype(F32)
    return hi, mid, lo


def _softplus(x):
    return jnp.maximum(x, 0.0) + jnp.log1p(jnp.exp(-jnp.abs(x)))


def _ssd_kernel(*refs, nc, zero_init, with_output):
    refs = list(refs)
    x_ref, b_ref, c_ref, dtt_ref, bias_ref, alog_ref = refs[:6]
    del refs[:6]
    z_ref, dskip_ref = (refs.pop(0), refs.pop(0)) if with_output else (None, None)
    init_ref = None if zero_init else refs.pop(0)
    u_ref, ssq_ref = (refs.pop(0), refs.pop(0)) if with_output else (None, None)
    fin_ref, state_ref, rhs_ref, decay_ref, exit_ref, entry_ref = refs[:6]
    y_ref = refs[6] if with_output else None
    q = SSD_CHUNK
    e_heads = HEADS_PER_GROUP
    wide = e_heads * q
    ii = lax.broadcasted_iota(jnp.int32, (q, q), 0)
    jj = lax.broadcasted_iota(jnp.int32, (q, q), 1)
    eye = (ii == jj).astype(BF16)
    left_b = (jj < SSD_HEAD_DIM).astype(BF16)
    right_b = (jj >= SSD_HEAD_DIM).astype(BF16)
    r32 = lax.broadcasted_iota(jnp.int32, (32, wide), 0)
    c32 = lax.broadcasted_iota(jnp.int32, (32, wide), 1) // q
    head_rows = ((r32 < 3 * e_heads) & ((r32 & (e_heads - 1)) == c32)).astype(F32)
    r8 = lax.broadcasted_iota(jnp.int32, (e_heads, wide), 0)
    c8 = lax.broadcasted_iota(jnp.int32, (e_heads, wide), 1) // q
    own_block = (r8 == c8).astype(F32)
    re = lax.broadcasted_iota(jnp.int32, (q, GROUP_WIDTH), 0)
    ce = lax.broadcasted_iota(jnp.int32, (q, GROUP_WIDTH), 1) // SSD_HEAD_DIM
    expand = ((re >= 64) & (re < 64 + 3 * e_heads) & ((re & (e_heads - 1)) == ce)).astype(BF16)
    zeros8 = jnp.zeros((e_heads, q), F32)
    ones24 = jnp.ones((3 * e_heads, q), F32)

    head_params = []
    for d in range(2):
        lo, hi = d * e_heads, (d + 1) * e_heads
        visible = (jj >= ii) if d else (jj <= ii)
        cum_mat = ((jj <= ii) if d else (jj >= ii)).astype(BF16)
        head_params.append((bias_ref[lo:hi, :], -jnp.exp(alog_ref[lo:hi, :]), cum_mat))

        @pl.when(pl.program_id(1) == 0)
        def _(d=d, visible=visible):
            rhs_ref[d, 0:q, :] = jnp.tile(jnp.where(visible, 0.0, NEG_BIG),
                                          (1, e_heads)).astype(BF16)
            rhs_ref[d, q:q + 32, :] = head_rows.astype(BF16)
            rhs_ref[d, q + 32:2 * q, :] = jnp.zeros((q - 32, wide), BF16)

        if zero_init:
            state_ref[d] = jnp.zeros(state_ref.shape[1:], F32)
        else:
            state_ref[d] = init_ref[d]

    def chunk_rows(d, step):
        c = (nc - 1 - step) if d else step
        return pl.ds(c * q if isinstance(c, int) else pl.multiple_of(c * q, q), q)

    def prepare_cumsum(d, step):
        lo, hi = d * e_heads, (d + 1) * e_heads
        bias_t, aneg_t, cum_mat = head_params[d]
        rows = chunk_rows(d, step)
        dt = _softplus(dtt_ref[lo:hi, rows] + bias_t)
        da = jnp.concatenate(_split3(dt * aneg_t), axis=0).astype(BF16)
        return dict(d=d, rows=rows, dt=dt, cs=jnp.dot(da, cum_mat, preferred_element_type=F32))

    def prepare_matmuls(v):
        d, dt, cs = v["d"], v["dt"], v["cs"]
        last = 0 if d else q - 1
        bc = b_ref[v["rows"], :]
        cc = c_ref[v["rows"], :]
        a2 = (cs[0:e_heads] + cs[e_heads:2 * e_heads] + cs[2 * e_heads:]) * LOG2E
        ea = jnp.exp2(a2)
        w_exit = jnp.exp2(a2[:, last:last + 1] - a2) * dt
        g = jnp.maximum(jnp.log2(dt), LOG2_DT_FLOOR) - a2
        rhs_ref[d, q + 32:q + 64, :] = jnp.concatenate(
            [jnp.tile(part, (1, e_heads)) * own_block for part in _split3(g)]
            + [jnp.zeros((e_heads, wide), F32)], axis=0).astype(BF16)
        split_t = jnp.concatenate(list(_split3(a2)) + [zeros8, ones24, zeros8]
                                  + list(_split3(ea)) + [jnp.zeros((q - 88, q), F32)], axis=0)
        split_n = split_t.T.astype(BF16)
        v["seg"] = jnp.dot(jnp.concatenate([eye, split_n], axis=1), rhs_ref[d],
                           preferred_element_type=F32)
        v["ea_wide"] = jnp.dot(split_n, expand, preferred_element_type=F32)
        v["cb"] = lax.dot_general(cc, bc, (((1,), (1,)), ((), ())), preferred_element_type=F32)
        b_t = bc.astype(F32).T
        v["b_exit"] = jnp.concatenate([b_t * w_exit[e:e + 1, :] for e in range(e_heads)],
                                      axis=1).astype(BF16)

    def prepare_store(v, slot):
        d = v["d"]
        decay_ref[slot, d] = (jnp.tile(v["cb"], (1, e_heads)) * jnp.exp2(v["seg"])).astype(BF16)
        exit_ref[slot, d] = v["b_exit"]
        entry_ref[slot, d] = v["ea_wide"]

    def emit_entry(d, step):
        rows = chunk_rows(d, step)
        state = state_ref[d]
        w = dict(d=d, rows=rows, state=state)
        if with_output:
            w["y_off"] = jnp.dot(c_ref[rows, :], state.astype(BF16), preferred_element_type=F32)
        return w

    def emit_outputs(w, slot, second_visit):
        d, rows = w["d"], w["rows"]
        last = 0 if d else q - 1
        xc = x_ref[rows, :]
        ssq = jnp.zeros((q, LANES), F32)
        for p in range(e_heads // 2):
            cols = slice(p * LANES, (p + 1) * LANES)
            two = slice(2 * p * q, 2 * (p + 1) * q)
            xp = xc[:, cols]
            x_pair = jnp.concatenate([xp * left_b, xp * right_b], axis=0)
            entry = entry_ref[slot, d, :, cols]
            if not with_output:
                new = jnp.dot(exit_ref[slot, d, :, two], x_pair, preferred_element_type=F32)
                state_ref[d, :, cols] = w["state"][:, cols] * entry[last:last + 1, :] + new
                continue
            lhs = jnp.concatenate([decay_ref[slot, d, :, two], exit_ref[slot, d, :, two]], axis=0)
            out = jnp.dot(lhs, x_pair, preferred_element_type=F32)
            state_ref[d, :, cols] = w["state"][:, cols] * entry[last:last + 1, :] + out[q:]
            y = out[:q] + w["y_off"][:, cols] * entry
            if not second_visit:
                y_ref[rows, cols] = y
                continue
            z = z_ref[rows, cols].astype(F32)
            y = y_ref[rows, cols] + y + xp.astype(F32) * dskip_ref[:, cols]
            gated = y * (z * jax.nn.sigmoid(z))
            u_ref[rows, cols] = gated.astype(u_ref.dtype)
            ssq = ssq + gated * gated
        if with_output and second_visit:
            ssq_ref[rows, :] = ssq

    def prepare_all(step, slot):
        chains = [prepare_cumsum(d, step) for d in range(2)]
        for v in chains:
            prepare_matmuls(v)
        for v in chains:
            prepare_store(v, slot)

    def body(second_visits, final=False):
        def run(it, carry):
            for half in range(2):
                step = 2 * it + half
                look_ahead = not (final and half)
                chains = [prepare_cumsum(d, step + 1) for d in range(2)] if look_ahead else []
                entries = [emit_entry(d, step) for d in range(2)]
                for n, w in enumerate(entries):
                    emit_outputs(w, half, second_visits[half])
                    if look_ahead:
                        prepare_matmuls(chains[n])
                for v in chains:
                    prepare_store(v, 1 - half)
            return carry
        return run

    prepare_all(0, 0)
    n_it = nc // 2
    if nc == 2:
        body((False, True), final=True)(0, 0)
    else:
        lax.fori_loop(0, n_it // 2, body((False, False)), 0)
        lax.fori_loop(n_it // 2, n_it - 1, body((True, True)), 0)
        body((True, True), final=True)(n_it - 1, 0)
    for d in range(2):
        fin_ref[d] = state_ref[d]


def ssd_scan(xbc, dt_t, bias_t, alog_t, init, gate=None):
    b, t, _ = xbc.shape
    nc = t // SSD_CHUNK
    assert nc == 2 or nc % 4 == 0
    wide = HEADS_PER_GROUP * SSD_CHUNK
    g = SSD_GROUPS
    e2 = 2 * HEADS_PER_GROUP
    bm0 = D_INNER // D_STATE
    cm0 = bm0 + g
    state_shape = (2, D_STATE, GROUP_WIDTH)
    group_block = pl.BlockSpec((None, t, GROUP_WIDTH), lambda i, j: (i, 0, j))
    state_block = pl.BlockSpec((None, None) + state_shape, lambda i, j: (i, j, 0, 0, 0))
    in_specs = [group_block,
                pl.BlockSpec((None, t, D_STATE), lambda i, j: (i, 0, bm0 + j)),
                pl.BlockSpec((None, t, D_STATE), lambda i, j: (i, 0, cm0 + j)),
                pl.BlockSpec((e2, t), lambda i, j: (j, i)),
                pl.BlockSpec((None, e2, 1), lambda i, j: (j, 0, 0)),
                pl.BlockSpec((None, e2, 1), lambda i, j: (j, 0, 0))]
    args = [xbc, xbc, xbc, dt_t, bias_t, alog_t]
    out_specs, out_shape = [state_block], [jax.ShapeDtypeStruct((b, g) + state_shape, F32)]
    scratch = [pltpu.VMEM(state_shape, F32),
               pltpu.VMEM((2, 2 * SSD_CHUNK, wide), BF16),
               pltpu.VMEM((2, 2, SSD_CHUNK, wide), BF16),
               pltpu.VMEM((2, 2, D_STATE, wide), BF16),
               pltpu.VMEM((2, 2, SSD_CHUNK, GROUP_WIDTH), F32)]
    if gate is not None:
        p, dskip = gate
        z0 = P_Z // GROUP_WIDTH
        in_specs += [pl.BlockSpec((None, t, GROUP_WIDTH), lambda i, j: (i, 0, z0 + j)),
                     pl.BlockSpec((None, 1, GROUP_WIDTH), lambda i, j: (j, 0, 0))]
        args += [p, dskip]
        out_specs = [group_block, pl.BlockSpec((None, t, LANES), lambda i, j: (i, 0, j))] + out_specs
        out_shape = [jax.ShapeDtypeStruct((b, t, D_INNER), BF16),
                     jax.ShapeDtypeStruct((b, t, g * LANES), F32)] + out_shape
        scratch.append(pltpu.VMEM((t, GROUP_WIDTH), F32))
    if init is not None:
        in_specs.append(state_block)
        args.append(init)
    outs = pl.pallas_call(
        functools.partial(_ssd_kernel, nc=nc, zero_init=init is None, with_output=gate is not None),
        grid=(b, g),
        in_specs=in_specs,
        out_specs=out_specs,
        out_shape=out_shape,
        scratch_shapes=scratch,
        compiler_params=_params("parallel", "arbitrary"),
        name="ssd_scan",
    )(*args)
    return tuple(outs) if gate is not None else (None, None, outs[0])


def _rope_tables(seq):
    t = jnp.arange(seq)
    row = (t // GRID_W).astype(F32)
    col = (t % GRID_W).astype(F32)
    axis_dim = HEAD_DIM // 2
    inv_freq = ROPE_THETA ** (-jnp.arange(0, axis_dim, 2, dtype=F32) / axis_dim)
    ang_r = row[:, None] * inv_freq[None]
    ang_c = col[:, None] * inv_freq[None]
    cos_r, sin_r, cos_c, sin_c = jnp.cos(ang_r), jnp.sin(ang_r), jnp.cos(ang_c), jnp.sin(ang_c)
    zero = jnp.zeros_like(sin_r)
    cos = jnp.concatenate([cos_r, cos_r, cos_c, cos_c], axis=-1)
    sin_next = jnp.concatenate([-sin_r, zero, -sin_c, zero], axis=-1)
    sin_prev = jnp.concatenate([zero, sin_r, zero, sin_c], axis=-1)
    return cos, sin_next, sin_prev


def _group_major(v):
    return v.reshape(2, SSD_GROUPS, HEADS_PER_GROUP).transpose(1, 0, 2).reshape(
        SSD_GROUPS, 2 * HEADS_PER_GROUP)


def _mixer_side(h, dt_t, p_cols_lo, w_main, conv_w, conv_b, ssd_params, init, dskip):
    b, t, d = h.shape
    h2 = h.reshape(b * t, d)
    n_out = P_COLS - p_cols_lo
    p = matmul([(h2, w_main)], [], lambda parts, ex: parts[0], n_out, BF16, tm=1024, tn=1024,
               w_col0=p_cols_lo, name="in_proj").reshape(b, t, n_out)
    xbc = conv_silu(p, -p_cols_lo, conv_w, conv_b)
    gate = None if dskip is None else (p, dskip)
    u, ssq, fin = ssd_scan(xbc, dt_t, *ssd_params, init, gate)
    return p, u, ssq, fin


def _merge_out(att, u, ssq, p, w_o_attn, w_o_ssd_g, w_out, x, mod, mod_row):
    b, t, d = x.shape
    m = b * t
    tm, tn = min(512, t), 1024
    p2 = p.reshape(m, p.shape[-1])
    ga0, gs0 = P_GATE_A // tn, P_GATE_S // tn
    n_part = ssq.shape[-1]

    def merge(parts, ex):
        ga, gs, sq = ex
        inv_rms = lax.rsqrt(jnp.sum(sq, axis=-1, keepdims=True) * (1.0 / D_INNER) + EPS)
        return (jax.nn.sigmoid(ga.astype(F32)) * parts[0]
                + jax.nn.sigmoid(gs.astype(F32)) * (inv_rms * parts[1]))

    merged = matmul([(att.reshape(m, ATT_WIDTH), w_o_attn), (u.reshape(m, D_INNER), w_o_ssd_g)],
                    [(p2, (tm, tn), lambda i, j: (i, ga0 + j)),
                     (p2, (tm, tn), lambda i, j: (i, gs0 + j)),
                     (ssq.reshape(m, n_part), (tm, n_part), lambda i, j: (i, 0))],
                    merge, d, BF16, tm=tm, tn=tn, w_stationary=True, name="merge")
    return _residual_matmul(merged, w_out, x, mod, mod_row, 2, name="out_proj")


def _residual_matmul(a, w, x, mod, mod_row, gate_idx, name):
    b, t, d = x.shape
    ktiled = a.shape[1] > 4096
    tm, tn = (min(1024, t), 1024) if ktiled else (min(512, t), d)
    per_seq = t // tm
    nj = d // tn
    extras = [(x.reshape(b * t, d), (tm, tn), lambda i, j: (i, j)),
              (mod, (None, 1, tn), lambda i, j: (mod_row(i // per_seq), 0, gate_idx * nj + j))]
    res = lambda parts, ex: ex[0] + ex[1] * parts[0]
    if ktiled:
        out = matmul_ktiled(a, w, extras, res, F32, tm=tm, tn=tn, tk=2048, name=name)
    else:
        out = matmul([(a, w)], extras, res, d, F32, tm=tm, tn=tn, name=name)
    return out.reshape(b, t, d)


def _mlp(x, g, w1, w2, mod, mod_row):
    b, t, d = x.shape
    h = norm_modulate(x, g, mod, mod_row, 3).reshape(b * t, d)
    hid = matmul([(h, w1)], [], lambda parts, ex: jnp.square(jnp.maximum(parts[0], 0.0)), D_FF,
                 BF16, tm=1024, tn=1024, name="ff1")
    return _residual_matmul(hid, w2, x, mod, mod_row, 5, name="ff2")


def kernel(x, c, ctx, c_ctx, w_ada, b_ada, g_norm1, g_norm2, w_in, attn_sink, conv_w, conv_b,
           dt_bias, a_log, d_skip, g_ssd, w_o_attn, w_o_ssd, w_out, w_ff1, w_ff2, g_final):
    depth = w_in.shape[0]
    batch, seq, d = x.shape
    q_scale = LOG2E * HEAD_DIM ** -0.5
    k_tables = _rope_tables(seq)
    q_tables = tuple(tb * q_scale for tb in k_tables)

    n_rows = -(-(batch + 1) // 8) * 8
    cond = jnp.zeros((n_rows, d), F32).at[:batch].set(c).at[batch].set(c_ctx)
    latent_row = lambda i: i
    ctx_row = lambda i: batch

    h_ctx = ctx
    for i in range(depth):
        ctx_out = i < depth - 1
        mod = adaln_modulation(cond, w_ada[i], b_ada[i][None]).reshape(n_rows, 1, N_MOD * d)

        wi = w_in[i]
        w_main = jnp.concatenate(
            [wi[:, REF_COL_Z:REF_COL_GATE], wi[:, REF_COL_GATE:], wi[:, REF_COL_Q:REF_COL_Z],
             wi[:, :REF_COL_DT]], axis=1).astype(BF16)
        w_dt_t = wi[:, REF_COL_DT:REF_COL_Q].reshape(d, 2, SSD_GROUPS, HEADS_PER_GROUP).transpose(
            2, 1, 3, 0).reshape(2 * SSD_HEADS, d).astype(BF16)
        ssd_params = (_group_major(dt_bias[i])[:, :, None], _group_major(a_log[i])[:, :, None])
        dskip = jnp.repeat(d_skip[i][0] + d_skip[i][1], SSD_HEAD_DIM).reshape(
            SSD_GROUPS, 1, GROUP_WIDTH)
        sink = attn_sink[i]
        wa, wo = w_o_attn[i].astype(BF16), w_out[i].astype(BF16)
        ws = (g_ssd[i][:, None] * w_o_ssd[i]).astype(BF16)
        w1, w2 = w_ff1[i].astype(BF16), w_ff2[i].astype(BF16)
        g1, g2 = g_norm1[i][None], g_norm2[i][None]

        ctx_lo = 0 if ctx_out else P_K
        hc, dt_c = norm_modulate(h_ctx, g1, mod, ctx_row, 0, w_dt_t)
        p_c, u_c, ssq_c, fin_c = _mixer_side(hc, dt_c, ctx_lo, w_main, conv_w[i], conv_b[i][None],
                                             ssd_params, None, dskip if ctx_out else None)
        hx, dt_x = norm_modulate(x, g1, mod, latent_row, 0, w_dt_t)
        p_x, u_x, ssq_x, _ = _mixer_side(hx, dt_x, 0, w_main, conv_w[i], conv_b[i][None],
                                         ssd_params, fin_c, dskip)
        att_x = latent_attention(p_x, p_c, -ctx_lo, sink, q_tables, k_tables)
        x = _merge_out(att_x, u_x, ssq_x, p_x, wa, ws, wo, x, mod, latent_row)
        x = _mlp(x, g2, w1, w2, mod, latent_row)
        if ctx_out:
            att_c = context_attention(p_c, sink)
            h_ctx = _merge_out(att_c, u_c, ssq_c, p_c, wa, ws, wo, h_ctx, mod, ctx_row)
            h_ctx = _mlp(h_ctx, g2, w1, w2, mod, ctx_row)
    return final_norm(x, g_final[None])
```

```python
import functools

import jax
import jax.numpy as jnp
from jax import lax
from jax.experimental import pallas as pl
from jax.experimental.pallas import tpu as pltpu

F32 = jnp.float32
BF16 = jnp.bfloat16

D_MODEL = 2048
GRID_W = 64
N_HEADS = 16
N_KV_HEADS = 4
HEAD_DIM = 128
REP = N_HEADS // N_KV_HEADS
ATT_WIDTH = N_HEADS * HEAD_DIM
KV_WIDTH = N_KV_HEADS * HEAD_DIM
WINDOW = 128
ATT_BLOCK = 128
ROPE_THETA = 10000.0
D_INNER = 2 * D_MODEL
SSD_HEAD_DIM = 64
SSD_HEADS = D_INNER // SSD_HEAD_DIM
SSD_GROUPS = 8
HEADS_PER_GROUP = SSD_HEADS // SSD_GROUPS
D_STATE = 128
CONV_K = 5
SSD_CHUNK = 128
GROUP_WIDTH = HEADS_PER_GROUP * SSD_HEAD_DIM
CONV_CH = D_INNER + 2 * SSD_GROUPS * D_STATE
D_FF = 4 * D_MODEL
N_MOD = 6
EPS = 1e-6
LOG2E = 1.4426950408889634
NEG_BIG = -1e30
LOG2_DT_FLOOR = -1e4

REF_COL_XBC = 2 * KV_WIDTH
REF_COL_DT = REF_COL_XBC + CONV_CH
REF_COL_Q = REF_COL_DT + 2 * SSD_HEADS
REF_COL_Z = REF_COL_Q + ATT_WIDTH
REF_COL_GATE = REF_COL_Z + D_INNER

P_Z = 0
P_GATE_A = P_Z + D_INNER
P_GATE_S = P_GATE_A + D_MODEL
P_Q = P_GATE_S + D_MODEL
P_K = P_Q + ATT_WIDTH
P_V = P_K + KV_WIDTH
P_XBC = P_V + KV_WIDTH
P_BM = P_XBC + D_INNER
P_CM = P_BM + SSD_GROUPS * D_STATE
P_COLS = P_CM + SSD_GROUPS * D_STATE

LANES = 128
VMEM_LIMIT = 56 * 1024 * 1024


def _params(*sem):
    return pltpu.CompilerParams(dimension_semantics=sem, vmem_limit_bytes=VMEM_LIMIT)


def _mm_kernel(*refs, n_pairs, n_extra, epilogue):
    a_refs = refs[:n_pairs]
    w_refs = refs[n_pairs:2 * n_pairs]
    extra = refs[2 * n_pairs:2 * n_pairs + n_extra]
    o_ref = refs[2 * n_pairs + n_extra]
    parts = [jnp.dot(a[...], w[...], preferred_element_type=F32) for a, w in zip(a_refs, w_refs)]
    o_ref[...] = epilogue(parts, [e[...] for e in extra]).astype(o_ref.dtype)


def matmul(pairs, extras, epilogue, n_out, out_dtype, tm, tn, w_col0=0, w_stationary=False,
           name="mm"):
    m = pairs[0][0].shape[0]
    tm = min(tm, m)
    assert m % tm == 0 and n_out % tn == 0 and w_col0 % tn == 0
    jo = w_col0 // tn
    if w_stationary:
        grid = (n_out // tn, m // tm)
        at = lambda im: (lambda j, i: im(i, j))
    else:
        grid = (m // tm, n_out // tn)
        at = lambda im: im
    in_specs = [pl.BlockSpec((tm, a.shape[1]), at(lambda i, j: (i, 0))) for a, _ in pairs]
    in_specs += [pl.BlockSpec((w.shape[0], tn), at(lambda i, j: (0, j + jo))) for _, w in pairs]
    in_specs += [pl.BlockSpec(bs, at(im)) for _, bs, im in extras]
    kern = functools.partial(_mm_kernel, n_pairs=len(pairs), n_extra=len(extras), epilogue=epilogue)
    return pl.pallas_call(
        kern,
        grid=grid,
        in_specs=in_specs,
        out_specs=pl.BlockSpec((tm, tn), at(lambda i, j: (i, j))),
        out_shape=jax.ShapeDtypeStruct((m, n_out), out_dtype),
        compiler_params=_params("parallel", "arbitrary"),
        name=name,
    )(*[a for a, _ in pairs], *[w for _, w in pairs], *[e for e, _, _ in extras])


def _mm_acc_kernel(*refs, nk, n_extra, epilogue):
    a_ref, w_ref = refs[0], refs[1]
    extra = refs[2:2 + n_extra]
    o_ref, acc_ref = refs[2 + n_extra], refs[3 + n_extra]
    k = pl.program_id(2)

    @pl.when(k == 0)
    def _():
        acc_ref[...] = jnp.zeros_like(acc_ref)

    acc_ref[...] += jnp.dot(a_ref[...], w_ref[...], preferred_element_type=F32)

    @pl.when(k == nk - 1)
    def _():
        o_ref[...] = epilogue([acc_ref[...]], [e[...] for e in extra]).astype(o_ref.dtype)


def matmul_ktiled(a, w, extras, epilogue, out_dtype, tm, tn, tk, name="mmk"):
    m, kdim = a.shape
    n_out = w.shape[1]
    tm = min(tm, m)
    assert m % tm == 0 and n_out % tn == 0 and kdim % tk == 0
    nk = kdim // tk
    in_specs = [pl.BlockSpec((tm, tk), lambda i, j, k: (i, k)),
                pl.BlockSpec((tk, tn), lambda i, j, k: (k, j))]
    in_specs += [pl.BlockSpec(bs, (lambda i, j, k, im=im: im(i, j))) for _, bs, im in extras]
    kern = functools.partial(_mm_acc_kernel, nk=nk, n_extra=len(extras), epilogue=epilogue)
    return pl.pallas_call(
        kern,
        grid=(m // tm, n_out // tn, nk),
        in_specs=in_specs,
        out_specs=pl.BlockSpec((tm, tn), lambda i, j, k: (i, j)),
        out_shape=jax.ShapeDtypeStruct((m, n_out), out_dtype),
        scratch_shapes=[pltpu.VMEM((tm, tn), F32)],
        compiler_params=_params("parallel", "arbitrary", "arbitrary"),
        name=name,
    )(a, w, *[e for e, _, _ in extras])


def _adaln_kernel(c_ref, w_ref, b_ref, o_ref):
    c = c_ref[...]
    silu_c = c * jax.nn.sigmoid(c)
    o_ref[...] = jnp.dot(silu_c, w_ref[...], preferred_element_type=F32,
                         precision=lax.Precision.HIGHEST) + b_ref[...]


def adaln_modulation(cond, w, b, tn=512):
    r, d = cond.shape
    n = w.shape[1]
    return pl.pallas_call(
        _adaln_kernel,
        grid=(n // tn,),
        in_specs=[pl.BlockSpec((r, d), lambda j: (0, 0)),
                  pl.BlockSpec((d, tn), lambda j: (0, j)),
                  pl.BlockSpec((1, tn), lambda j: (0, j))],
        out_specs=pl.BlockSpec((r, tn), lambda j: (0, j)),
        out_shape=jax.ShapeDtypeStruct((r, n), F32),
        compiler_params=_params("arbitrary"),
        name="adaln",
    )(cond, w, b)


def _norm_mod_kernel(x_ref, g_ref, shift_ref, scale_ref, *rest):
    x = x_ref[...]
    y = x * lax.rsqrt(jnp.mean(x * x, axis=-1, keepdims=True) + EPS) * g_ref[...]
    h = (y * (1.0 + scale_ref[...]) + shift_ref[...]).astype(BF16)
    if len(rest) == 1:
        rest[0][...] = h
        return
    wt_ref, o_ref, dt_ref = rest
    o_ref[...] = h
    dt_ref[...] = lax.dot_general(wt_ref[...], h, (((1,), (1,)), ((), ())),
                                  preferred_element_type=F32)


def norm_modulate(x, g, mod, mod_row, shift_idx, w_dt_t=None, ts=512):
    b, t, d = x.shape
    ts = min(ts, t)
    per_seq = t // ts
    in_specs = [pl.BlockSpec((None, ts, d), lambda i, j: (i, j, 0)),
                pl.BlockSpec((1, d), lambda i, j: (0, 0)),
                pl.BlockSpec((None, 1, d), lambda i, j: (mod_row(i), 0, shift_idx)),
                pl.BlockSpec((None, 1, d), lambda i, j: (mod_row(i), 0, shift_idx + 1))]
    out_specs = pl.BlockSpec((None, ts, d), lambda i, j: (i, j, 0))
    out_shape = jax.ShapeDtypeStruct((b, t, d), BF16)
    args = [x, g, mod, mod]
    if w_dt_t is not None:
        in_specs.append(pl.BlockSpec((LANES, d), lambda i, j: (0, 0)))
        out_specs = [out_specs, pl.BlockSpec((LANES, ts), lambda i, j: (0, i * per_seq + j))]
        out_shape = [out_shape, jax.ShapeDtypeStruct((LANES, b * t), F32)]
        args.append(w_dt_t)
    return pl.pallas_call(
        _norm_mod_kernel,
        grid=(b, t // ts),
        in_specs=in_specs,
        out_specs=out_specs,
        out_shape=out_shape,
        compiler_params=_params("parallel", "parallel"),
        name="norm_mod",
    )(*args)


def _final_norm_kernel(x_ref, g_ref, o_ref):
    x = x_ref[...]
    o_ref[...] = x * lax.rsqrt(jnp.mean(x * x, axis=-1, keepdims=True) + EPS) * g_ref[...]


def final_norm(x, g, ts=512):
    b, t, d = x.shape
    ts = min(ts, t)
    return pl.pallas_call(
        _final_norm_kernel,
        grid=(b, t // ts),
        in_specs=[pl.BlockSpec((None, ts, d), lambda i, j: (i, j, 0)),
                  pl.BlockSpec((1, d), lambda i, j: (0, 0))],
        out_specs=pl.BlockSpec((None, ts, d), lambda i, j: (i, j, 0)),
        out_shape=jax.ShapeDtypeStruct((b, t, d), F32),
        compiler_params=_params("parallel", "parallel"),
        name="final_norm",
    )(x, g)


CONV_ROWS = 128
CONV_HALO = 16
CONV_UNROLL = 8


def _conv_silu_kernel(u_ref, w_ref, b_ref, o_ref, ext_ref, *, unroll):
    t, tc = u_ref.shape
    pad = CONV_K // 2
    span = CONV_ROWS + 2 * CONV_HALO
    halo = jnp.zeros((CONV_HALO, tc), BF16)
    ext_ref[0:CONV_HALO, :] = halo
    ext_ref[CONV_HALO + t:, :] = halo
    ext_ref[CONV_HALO:CONV_HALO + t, :] = u_ref[...]
    taps = [k - pad for k in range(CONV_K) if k != pad]
    ri = lax.broadcasted_iota(jnp.int32, (len(taps) * CONV_ROWS, span), 0)
    ci = lax.broadcasted_iota(jnp.int32, (len(taps) * CONV_ROWS, span), 1)
    tap_off = jnp.zeros_like(ri)
    for n, off in enumerate(taps):
        tap_off = jnp.where(ri // CONV_ROWS == n, off, tap_off)
    shift_mat = (ci == (ri % CONV_ROWS) + CONV_HALO + tap_off).astype(BF16)
    bias = b_ref[...]
    w = [w_ref[k:k + 1, :] for k in range(CONV_K)]

    def body(it, carry):
        windows, shifted = [], []
        for n in range(unroll):
            r0 = pl.multiple_of((it * unroll + n) * CONV_ROWS, CONV_ROWS)
            win = ext_ref[pl.ds(r0, span), :]
            windows.append((r0, win))
            shifted.append(jnp.dot(shift_mat, win, preferred_element_type=F32))
        for (r0, win), sh in zip(windows, shifted):
            acc = bias + w[pad] * win[CONV_HALO:CONV_HALO + CONV_ROWS].astype(F32)
            for n, off in enumerate(taps):
                acc = acc + w[pad + off] * sh[n * CONV_ROWS:(n + 1) * CONV_ROWS]
            o_ref[pl.ds(r0, CONV_ROWS), :] = (acc * jax.nn.sigmoid(acc)).astype(o_ref.dtype)
        return carry

    lax.fori_loop(0, t // (CONV_ROWS * unroll), body, 0)


def conv_silu(p, col0, conv_w, conv_b):
    b, t, _ = p.shape
    tc = 512 if t >= 1024 else 1024
    unroll = min(CONV_UNROLL, t // CONV_ROWS)
    assert t % (CONV_ROWS * unroll) == 0
    c0 = (col0 + P_XBC) // tc
    return pl.pallas_call(
        functools.partial(_conv_silu_kernel, unroll=unroll),
        grid=(b, CONV_CH // tc),
        in_specs=[pl.BlockSpec((None, t, tc), lambda i, j: (i, 0, c0 + j)),
                  pl.BlockSpec((CONV_K, tc), lambda i, j: (0, j)),
                  pl.BlockSpec((1, tc), lambda i, j: (0, j))],
        out_specs=pl.BlockSpec((None, t, tc), lambda i, j: (i, 0, j)),
        out_shape=jax.ShapeDtypeStruct((b, t, CONV_CH), BF16),
        scratch_shapes=[pltpu.VMEM((t + 2 * CONV_HALO, tc), BF16)],
        compiler_params=_params("parallel", "parallel"),
        name="conv_silu",
    )(p, conv_w, conv_b)


def _rope(u, cos, sin_next, sin_prev):
    quarter = HEAD_DIM // 4
    return (u * cos + pltpu.roll(u, HEAD_DIM - quarter, axis=1) * sin_next
            + pltpu.roll(u, quarter, axis=1) * sin_prev)


def _softmax_pv(s, sink_col, v):
    m = jnp.maximum(jnp.max(s, axis=-1, keepdims=True), sink_col)
    e = jnp.exp(s - m)
    denom = jnp.sum(e, axis=-1, keepdims=True) + jnp.exp(sink_col - m)
    o = jnp.dot(e.astype(BF16), v, preferred_element_type=F32)
    return o / denom


def _sink_column(sink_ref, h, rows):
    blk = lax.broadcasted_iota(jnp.int32, (REP * rows, 1), 0) // rows
    col = jnp.zeros((REP * rows, 1), F32)
    for r in range(REP):
        col = jnp.where(blk == r, sink_ref[h * REP + r], col)
    return col


def _latent_attn_kernel(sink_ref, q_ref, kp_ref, kc_ref, kn_ref, vp_ref, vc_ref, vn_ref,
                        kx_ref, vx_ref, qcos_ref, qsn_ref, qsp_ref, cos_ref, sn_ref, sp_ref, o_ref,
                        *, nb):
    n = pl.program_id(1)
    blk = ATT_BLOCK

    def rows(ref, i):
        return ref[pl.ds(pl.multiple_of(i * blk, blk), blk), :]

    n_prev = jnp.maximum(n - 1, 0)
    n_next = jnp.minimum(n + 1, nb - 1)
    k_tables = [[rows(ref, i) for ref in (cos_ref, sn_ref, sp_ref)] for i in (n_prev, n, n_next)]
    qcos, qsn, qsp = rows(qcos_ref, n), rows(qsn_ref, n), rows(qsp_ref, n)
    qi = lax.broadcasted_iota(jnp.int32, (blk, blk), 0)
    kj = lax.broadcasted_iota(jnp.int32, (blk, blk), 1)
    see_prev = kj >= qi + (n == 0).astype(jnp.int32) * blk
    see_next = kj <= qi - (n == nb - 1).astype(jnp.int32) * blk
    for h in range(N_KV_HEADS):
        kv = slice(h * HEAD_DIM, (h + 1) * HEAD_DIM)
        keys = jnp.concatenate(
            [_rope(ref[:, kv].astype(F32), *tabs).astype(BF16)
             for ref, tabs in zip((kp_ref, kc_ref, kn_ref), k_tables)] + [kx_ref[:, kv]], axis=0)
        vals = jnp.concatenate([vp_ref[:, kv], vc_ref[:, kv], vn_ref[:, kv], vx_ref[:, kv]], axis=0)
        vals = jnp.concatenate([vals, jnp.ones_like(vals)], axis=1)
        heads = [slice((h * REP + r) * HEAD_DIM, (h * REP + r + 1) * HEAD_DIM) for r in range(REP)]
        logits = []
        for cols in heads:
            q = _rope(q_ref[:, cols].astype(F32), qcos, qsn, qsp).astype(BF16)
            logits.append(lax.dot_general(q, keys, (((1,), (1,)), ((), ())),
                                          preferred_element_type=F32))
        probs, sink_terms = [], []
        for r, s in enumerate(logits):
            s = jnp.concatenate(
                [jnp.where(see_prev, s[:, :blk], -jnp.inf), s[:, blk:2 * blk],
                 jnp.where(see_next, s[:, 2 * blk:3 * blk], -jnp.inf), s[:, 3 * blk:]], axis=1)
            sink = sink_ref[h * REP + r] * LOG2E
            m = jnp.maximum(jnp.max(s, axis=-1, keepdims=True), sink)
            probs.append(jnp.exp2(s - m).astype(BF16))
            sink_terms.append(jnp.exp2(sink - m))
        for cols, e, sink_term in zip(heads, probs, sink_terms):
            o = jnp.dot(e, vals, preferred_element_type=F32)
            o_ref[:, cols] = (o[:, :HEAD_DIM] / (o[:, HEAD_DIM:] + sink_term)).astype(o_ref.dtype)


def latent_attention(p, p_ctx, ctx_col0, sink, q_tables, k_tables):
    b, s, _ = p.shape
    l = p_ctx.shape[1]
    nb = s // ATT_BLOCK
    blk = ATT_BLOCK
    kb, vb = P_K // KV_WIDTH, P_V // KV_WIDTH
    kxb, vxb = (ctx_col0 + P_K) // KV_WIDTH, (ctx_col0 + P_V) // KV_WIDTH
    prev = lambda n: jnp.maximum(n - 1, 0)
    nxt = lambda n: jnp.minimum(n + 1, nb - 1)
    kv_spec = lambda cb, f: pl.BlockSpec((None, blk, KV_WIDTH), lambda i, n: (i, f(n), cb))
    table = pl.BlockSpec((s, HEAD_DIM), lambda i, n: (0, 0))
    return pl.pallas_call(
        functools.partial(_latent_attn_kernel, nb=nb),
        grid=(b, nb),
        in_specs=[pl.BlockSpec(memory_space=pltpu.SMEM),
                  pl.BlockSpec((None, blk, ATT_WIDTH), lambda i, n: (i, n, P_Q // ATT_WIDTH)),
                  kv_spec(kb, prev), kv_spec(kb, lambda n: n), kv_spec(kb, nxt),
                  kv_spec(vb, prev), kv_spec(vb, lambda n: n), kv_spec(vb, nxt),
                  pl.BlockSpec((None, l, KV_WIDTH), lambda i, n: (i, 0, kxb)),
                  pl.BlockSpec((None, l, KV_WIDTH), lambda i, n: (i, 0, vxb)),
                  table, table, table, table, table, table],
        out_specs=pl.BlockSpec((None, blk, ATT_WIDTH), lambda i, n: (i, n, 0)),
        out_shape=jax.ShapeDtypeStruct((b, s, ATT_WIDTH), BF16),
        compiler_params=_params("parallel", "parallel"),
        name="latent_attn",
    )(sink, p, p, p, p, p, p, p, p_ctx, p_ctx, *q_tables, *k_tables)


def _context_attn_kernel(sink_ref, q_ref, k_ref, v_ref, o_ref):
    h = pl.program_id(2)
    blk = q_ref.shape[0]
    q = jnp.concatenate([q_ref[:, r * HEAD_DIM:(r + 1) * HEAD_DIM] for r in range(REP)], axis=0)
    s = lax.dot_general(q, k_ref[...], (((1,), (1,)), ((), ())), preferred_element_type=F32)
    o = _softmax_pv(s * HEAD_DIM ** -0.5, _sink_column(sink_ref, h, blk), v_ref[...])
    for r in range(REP):
        o_ref[:, r * HEAD_DIM:(r + 1) * HEAD_DIM] = o[r * blk:(r + 1) * blk].astype(o_ref.dtype)


def context_attention(p_ctx, sink):
    b, l, _ = p_ctx.shape
    blk = min(ATT_BLOCK, l)
    kb, vb = P_K // HEAD_DIM, P_V // HEAD_DIM
    qb = P_Q // (REP * HEAD_DIM)
    return pl.pallas_call(
        _context_attn_kernel,
        grid=(b, l // blk, N_KV_HEADS),
        in_specs=[pl.BlockSpec(memory_space=pltpu.SMEM),
                  pl.BlockSpec((None, blk, REP * HEAD_DIM), lambda i, n, h: (i, n, qb + h)),
                  pl.BlockSpec((None, l, HEAD_DIM), lambda i, n, h: (i, 0, kb + h)),
                  pl.BlockSpec((None, l, HEAD_DIM), lambda i, n, h: (i, 0, vb + h))],
        out_specs=pl.BlockSpec((None, blk, REP * HEAD_DIM), lambda i, n, h: (i, n, h)),
        out_shape=jax.ShapeDtypeStruct((b, l, ATT_WIDTH), BF16),
        compiler_params=_params("parallel", "parallel", "arbitrary"),
        name="context_attn",
    )(sink, p_ctx, p_ctx, p_ctx)


def _split3(x):
    hi = x.astype(BF16).astype(F32)
    r1 = x - hi
    mid = r1.astype(BF16).astype(F32)
    lo = (r1 - mid).astype(BF16).astype(F32)
    return hi, mid, lo


def _softplus(x):
    return jnp.maximum(x, 0.0) + jnp.log1p(jnp.exp(-jnp.abs(x)))


def _ssd_kernel(*refs, nc, zero_init, with_output):
    refs = list(refs)
    x_ref, b_ref, c_ref, dtt_ref, bias_ref, alog_ref = refs[:6]
    del refs[:6]
    z_ref, dskip_ref = (refs.pop(0), refs.pop(0)) if with_output else (None, None)
    init_ref = None if zero_init else refs.pop(0)
    u_ref, ssq_ref = (refs.pop(0), refs.pop(0)) if with_output else (None, None)
    fin_ref, state_ref, rhs_ref, decay_ref, exit_ref, entry_ref = refs[:6]
    y_ref = refs[6] if with_output else None
    q = SSD_CHUNK
    e_heads = HEADS_PER_GROUP
    wide = e_heads * q
    ii = lax.broadcasted_iota(jnp.int32, (q, q), 0)
    jj = lax.broadcasted_iota(jnp.int32, (q, q), 1)
    eye = (ii == jj).astype(BF16)
    left_b = (jj < SSD_HEAD_DIM).astype(BF16)
    right_b = (jj >= SSD_HEAD_DIM).astype(BF16)
    r32 = lax.broadcasted_iota(jnp.int32, (32, wide), 0)
    c32 = lax.broadcasted_iota(jnp.int32, (32, wide), 1) // q
    head_rows = ((r32 < 3 * e_heads) & ((r32 & (e_heads - 1)) == c32)).astype(F32)
    r8 = lax.broadcasted_iota(jnp.int32, (e_heads, wide), 0)
    c8 = lax.broadcasted_iota(jnp.int32, (e_heads, wide), 1) // q
    own_block = (r8 == c8).astype(F32)
    re = lax.broadcasted_iota(jnp.int32, (q, GROUP_WIDTH), 0)
    ce = lax.broadcasted_iota(jnp.int32, (q, GROUP_WIDTH), 1) // SSD_HEAD_DIM
    expand = ((re >= 64) & (re < 64 + 3 * e_heads) & ((re & (e_heads - 1)) == ce)).astype(BF16)
    zeros8 = jnp.zeros((e_heads, q), F32)
    ones24 = jnp.ones((3 * e_heads, q), F32)

    head_params = []
    for d in range(2):
        lo, hi = d * e_heads, (d + 1) * e_heads
        visible = (jj >= ii) if d else (jj <= ii)
        cum_mat = ((jj <= ii) if d else (jj >= ii)).astype(BF16)
        head_params.append((bias_ref[lo:hi, :], -jnp.exp(alog_ref[lo:hi, :]), cum_mat))

        @pl.when(pl.program_id(1) == 0)
        def _(d=d, visible=visible):
            rhs_ref[d, 0:q, :] = jnp.tile(jnp.where(visible, 0.0, NEG_BIG),
                                          (1, e_heads)).astype(BF16)
            rhs_ref[d, q:q + 32, :] = head_rows.astype(BF16)
            rhs_ref[d, q + 32:2 * q, :] = jnp.zeros((q - 32, wide), BF16)

        if zero_init:
            state_ref[d] = jnp.zeros(state_ref.shape[1:], F32)
        else:
            state_ref[d] = init_ref[d]

    def chunk_rows(d, step):
        c = (nc - 1 - step) if d else step
        return pl.ds(c * q if isinstance(c, int) else pl.multiple_of(c * q, q), q)

    def prepare_cumsum(d, step):
        lo, hi = d * e_heads, (d + 1) * e_heads
        bias_t, aneg_t, cum_mat = head_params[d]
        rows = chunk_rows(d, step)
        dt = _softplus(dtt_ref[lo:hi, rows] + bias_t)
        da = jnp.concatenate(_split3(dt * aneg_t), axis=0).astype(BF16)
        return dict(d=d, rows=rows, dt=dt, cs=jnp.dot(da, cum_mat, preferred_element_type=F32))

    def prepare_matmuls(v):
        d, dt, cs = v["d"], v["dt"], v["cs"]
        last = 0 if d else q - 1
        bc = b_ref[v["rows"], :]
        cc = c_ref[v["rows"], :]
        a2 = (cs[0:e_heads] + cs[e_heads:2 * e_heads] + cs[2 * e_heads:]) * LOG2E
        ea = jnp.exp2(a2)
        w_exit = jnp.exp2(a2[:, last:last + 1] - a2) * dt
        g = jnp.maximum(jnp.log2(dt), LOG2_DT_FLOOR) - a2
        rhs_ref[d, q + 32:q + 64, :] = jnp.concatenate(
            [jnp.tile(part, (1, e_heads)) * own_block for part in _split3(g)]
            + [jnp.zeros((e_heads, wide), F32)], axis=0).astype(BF16)
        split_t = jnp.concatenate(list(_split3(a2)) + [zeros8, ones24, zeros8]
                                  + list(_split3(ea)) + [jnp.zeros((q - 88, q), F32)], axis=0)
        split_n = split_t.T.astype(BF16)
        v["seg"] = jnp.dot(jnp.concatenate([eye, split_n], axis=1), rhs_ref[d],
                           preferred_element_type=F32)
        v["ea_wide"] = jnp.dot(split_n, expand, preferred_element_type=F32)
        v["cb"] = lax.dot_general(cc, bc, (((1,), (1,)), ((), ())), preferred_element_type=F32)
        b_t = bc.astype(F32).T
        v["b_exit"] = jnp.concatenate([b_t * w_exit[e:e + 1, :] for e in range(e_heads)],
                                      axis=1).astype(BF16)

    def prepare_store(v, slot):
        d = v["d"]
        decay_ref[slot, d] = (jnp.tile(v["cb"], (1, e_heads)) * jnp.exp2(v["seg"])).astype(BF16)
        exit_ref[slot, d] = v["b_exit"]
        entry_ref[slot, d] = v["ea_wide"]

    def emit_entry(d, step):
        rows = chunk_rows(d, step)
        state = state_ref[d]
        w = dict(d=d, rows=rows, state=state)
        if with_output:
            w["y_off"] = jnp.dot(c_ref[rows, :], state.astype(BF16), preferred_element_type=F32)
        return w

    def emit_outputs(w, slot, second_visit):
        d, rows = w["d"], w["rows"]
        last = 0 if d else q - 1
        xc = x_ref[rows, :]
        ssq = jnp.zeros((q, LANES), F32)
        for p in range(e_heads // 2):
            cols = slice(p * LANES, (p + 1) * LANES)
            two = slice(2 * p * q, 2 * (p + 1) * q)
            xp = xc[:, cols]
            x_pair = jnp.concatenate([xp * left_b, xp * right_b], axis=0)
            entry = entry_ref[slot, d, :, cols]
            if not with_output:
                new = jnp.dot(exit_ref[slot, d, :, two], x_pair, preferred_element_type=F32)
                state_ref[d, :, cols] = w["state"][:, cols] * entry[last:last + 1, :] + new
                continue
            lhs = jnp.concatenate([decay_ref[slot, d, :, two], exit_ref[slot, d, :, two]], axis=0)
            out = jnp.dot(lhs, x_pair, preferred_element_type=F32)
            state_ref[d, :, cols] = w["state"][:, cols] * entry[last:last + 1, :] + out[q:]
            y = out[:q] + w["y_off"][:, cols] * entry
            if not second_visit:
                y_ref[rows, cols] = y
                continue
            z = z_ref[rows, cols].astype(F32)
            y = y_ref[rows, cols] + y + xp.astype(F32) * dskip_ref[:, cols]
            gated = y * (z * jax.nn.sigmoid(z))
            u_ref[rows, cols] = gated.astype(u_ref.dtype)
            ssq = ssq + gated * gated
        if with_output and second_visit:
            ssq_ref[rows, :] = ssq

    def prepare_all(step, slot):
        chains = [prepare_cumsum(d, step) for d in range(2)]
        for v in chains:
            prepare_matmuls(v)
        for v in chains:
            prepare_store(v, slot)

    def body(second_visits, final=False):
        def run(it, carry):
            for half in range(2):
                step = 2 * it + half
                look_ahead = not (final and half)
                chains = [prepare_cumsum(d, step + 1) for d in range(2)] if look_ahead else []
                entries = [emit_entry(d, step) for d in range(2)]
                for n, w in enumerate(entries):
                    emit_outputs(w, half, second_visits[half])
                    if look_ahead:
                        prepare_matmuls(chains[n])
                for v in chains:
                    prepare_store(v, 1 - half)
            return carry
        return run

    prepare_all(0, 0)
    n_it = nc // 2
    if nc == 2:
        body((False, True), final=True)(0, 0)
    else:
        lax.fori_loop(0, n_it // 2, body((False, False)), 0)
        lax.fori_loop(n_it // 2, n_it - 1, body((True, True)), 0)
        body((True, True), final=True)(n_it - 1, 0)
    for d in range(2):
        fin_ref[d] = state_ref[d]


def ssd_scan(xbc, dt_t, bias_t, alog_t, init, gate=None):
    b, t, _ = xbc.shape
    nc = t // SSD_CHUNK
    assert nc == 2 or nc % 4 == 0
    wide = HEADS_PER_GROUP * SSD_CHUNK
    g = SSD_GROUPS
    e2 = 2 * HEADS_PER_GROUP
    bm0 = D_INNER // D_STATE
    cm0 = bm0 + g
    state_shape = (2, D_STATE, GROUP_WIDTH)
    group_block = pl.BlockSpec((None, t, GROUP_WIDTH), lambda i, j: (i, 0, j))
    state_block = pl.BlockSpec((None, None) + state_shape, lambda i, j: (i, j, 0, 0, 0))
    in_specs = [group_block,
                pl.BlockSpec((None, t, D_STATE), lambda i, j: (i, 0, bm0 + j)),
                pl.BlockSpec((None, t, D_STATE), lambda i, j: (i, 0, cm0 + j)),
                pl.BlockSpec((e2, t), lambda i, j: (j, i)),
                pl.BlockSpec((None, e2, 1), lambda i, j: (j, 0, 0)),
                pl.BlockSpec((None, e2, 1), lambda i, j: (j, 0, 0))]
    args = [xbc, xbc, xbc, dt_t, bias_t, alog_t]
    out_specs, out_shape = [state_block], [jax.ShapeDtypeStruct((b, g) + state_shape, F32)]
    scratch = [pltpu.VMEM(state_shape, F32),
               pltpu.VMEM((2, 2 * SSD_CHUNK, wide), BF16),
               pltpu.VMEM((2, 2, SSD_CHUNK, wide), BF16),
               pltpu.VMEM((2, 2, D_STATE, wide), BF16),
               pltpu.VMEM((2, 2, SSD_CHUNK, GROUP_WIDTH), F32)]
    if gate is not None:
        p, dskip = gate
        z0 = P_Z // GROUP_WIDTH
        in_specs += [pl.BlockSpec((None, t, GROUP_WIDTH), lambda i, j: (i, 0, z0 + j)),
                     pl.BlockSpec((None, 1, GROUP_WIDTH), lambda i, j: (j, 0, 0))]
        args += [p, dskip]
        out_specs = [group_block, pl.BlockSpec((None, t, LANES), lambda i, j: (i, 0, j))] + out_specs
        out_shape = [jax.ShapeDtypeStruct((b, t, D_INNER), BF16),
                     jax.ShapeDtypeStruct((b, t, g * LANES), F32)] + out_shape
        scratch.append(pltpu.VMEM((t, GROUP_WIDTH), F32))
    if init is not None:
        in_specs.append(state_block)
        args.append(init)
    outs = pl.pallas_call(
        functools.partial(_ssd_kernel, nc=nc, zero_init=init is None, with_output=gate is not None),
        grid=(b, g),
        in_specs=in_specs,
        out_specs=out_specs,
        out_shape=out_shape,
        scratch_shapes=scratch,
        compiler_params=_params("parallel", "arbitrary"),
        name="ssd_scan",
    )(*args)
    return tuple(outs) if gate is not None else (None, None, outs[0])


def _rope_tables(seq):
    t = jnp.arange(seq)
    row = (t // GRID_W).astype(F32)
    col = (t % GRID_W).astype(F32)
    axis_dim = HEAD_DIM // 2
    inv_freq = ROPE_THETA ** (-jnp.arange(0, axis_dim, 2, dtype=F32) / axis_dim)
    ang_r = row[:, None] * inv_freq[None]
    ang_c = col[:, None] * inv_freq[None]
    cos_r, sin_r, cos_c, sin_c = jnp.cos(ang_r), jnp.sin(ang_r), jnp.cos(ang_c), jnp.sin(ang_c)
    zero = jnp.zeros_like(sin_r)
    cos = jnp.concatenate([cos_r, cos_r, cos_c, cos_c], axis=-1)
    sin_next = jnp.concatenate([-sin_r, zero, -sin_c, zero], axis=-1)
    sin_prev = jnp.concatenate([zero, sin_r, zero, sin_c], axis=-1)
    return cos, sin_next, sin_prev


def _group_major(v):
    return v.reshape(2, SSD_GROUPS, HEADS_PER_GROUP).transpose(1, 0, 2).reshape(
        SSD_GROUPS, 2 * HEADS_PER_GROUP)


def _mixer_side(h, dt_t, p_cols_lo, w_main, conv_w, conv_b, ssd_params, init, dskip):
    b, t, d = h.shape
    h2 = h.reshape(b * t, d)
    n_out = P_COLS - p_cols_lo
    p = matmul([(h2, w_main)], [], lambda parts, ex: parts[0], n_out, BF16, tm=1024, tn=1024,
               w_col0=p_cols_lo, name="in_proj").reshape(b, t, n_out)
    xbc = conv_silu(p, -p_cols_lo, conv_w, conv_b)
    gate = None if dskip is None else (p, dskip)
    u, ssq, fin = ssd_scan(xbc, dt_t, *ssd_params, init, gate)
    return p, u, ssq, fin


def _merge_out(att, u, ssq, p, w_o_attn, w_o_ssd_g, w_out, x, mod, mod_row):
    b, t, d = x.shape
    m = b * t
    tm, tn = min(512, t), 1024
    p2 = p.reshape(m, p.shape[-1])
    ga0, gs0 = P_GATE_A // tn, P_GATE_S // tn
    n_part = ssq.shape[-1]

    def merge(parts, ex):
        ga, gs, sq = ex
        inv_rms = lax.rsqrt(jnp.sum(sq, axis=-1, keepdims=True) * (1.0 / D_INNER) + EPS)
        return (jax.nn.sigmoid(ga.astype(F32)) * parts[0]
                + jax.nn.sigmoid(gs.astype(F32)) * (inv_rms * parts[1]))

    merged = matmul([(att.reshape(m, ATT_WIDTH), w_o_attn), (u.reshape(m, D_INNER), w_o_ssd_g)],
                    [(p2, (tm, tn), lambda i, j: (i, ga0 + j)),
                     (p2, (tm, tn), lambda i, j: (i, gs0 + j)),
                     (ssq.reshape(m, n_part), (tm, n_part), lambda i, j: (i, 0))],
                    merge, d, BF16, tm=tm, tn=tn, w_stationary=True, name="merge")
    return _residual_matmul(merged, w_out, x, mod, mod_row, 2, name="out_proj")


def _residual_matmul(a, w, x, mod, mod_row, gate_idx, name):
    b, t, d = x.shape
    ktiled = a.shape[1] > 4096
    tm, tn = (min(1024, t), 1024) if ktiled else (min(512, t), d)
    per_seq = t // tm
    nj = d // tn
    extras = [(x.reshape(b * t, d), (tm, tn), lambda i, j: (i, j)),
              (mod, (None, 1, tn), lambda i, j: (mod_row(i // per_seq), 0, gate_idx * nj + j))]
    res = lambda parts, ex: ex[0] + ex[1] * parts[0]
    if ktiled:
        out = matmul_ktiled(a, w, extras, res, F32, tm=tm, tn=tn, tk=2048, name=name)
    else:
        out = matmul([(a, w)], extras, res, d, F32, tm=tm, tn=tn, name=name)
    return out.reshape(b, t, d)


def _mlp(x, g, w1, w2, mod, mod_row):
    b, t, d = x.shape
    h = norm_modulate(x, g, mod, mod_row, 3).reshape(b * t, d)
    hid = matmul([(h, w1)], [], lambda parts, ex: jnp.square(jnp.maximum(parts[0], 0.0)), D_FF,
                 BF16, tm=1024, tn=1024, name="ff1")
    return _residual_matmul(hid, w2, x, mod, mod_row, 5, name="ff2")


def kernel(x, c, ctx, c_ctx, w_ada, b_ada, g_norm1, g_norm2, w_in, attn_sink, conv_w, conv_b,
           dt_bias, a_log, d_skip, g_ssd, w_o_attn, w_o_ssd, w_out, w_ff1, w_ff2, g_final):
    depth = w_in.shape[0]
    batch, seq, d = x.shape
    q_scale = LOG2E * HEAD_DIM ** -0.5
    k_tables = _rope_tables(seq)
    q_tables = tuple(tb * q_scale for tb in k_tables)

    n_rows = -(-(batch + 1) // 8) * 8
    cond = jnp.zeros((n_rows, d), F32).at[:batch].set(c).at[batch].set(c_ctx)
    latent_row = lambda i: i
    ctx_row = lambda i: batch

    h_ctx = ctx
    for i in range(depth):
        ctx_out = i < depth - 1
        mod = adaln_modulation(cond, w_ada[i], b_ada[i][None]).reshape(n_rows, 1, N_MOD * d)

        wi = w_in[i]
        w_main = jnp.concatenate(
            [wi[:, REF_COL_Z:REF_COL_GATE], wi[:, REF_COL_GATE:], wi[:, REF_COL_Q:REF_COL_Z],
             wi[:, :REF_COL_DT]], axis=1).astype(BF16)
        w_dt_t = wi[:, REF_COL_DT:REF_COL_Q].reshape(d, 2, SSD_GROUPS, HEADS_PER_GROUP).transpose(
            2, 1, 3, 0).reshape(2 * SSD_HEADS, d).astype(BF16)
        ssd_params = (_group_major(dt_bias[i])[:, :, None], _group_major(a_log[i])[:, :, None])
        dskip = jnp.repeat(d_skip[i][0] + d_skip[i][1], SSD_HEAD_DIM).reshape(
            SSD_GROUPS, 1, GROUP_WIDTH)
        sink = attn_sink[i]
        wa, wo = w_o_attn[i].astype(BF16), w_out[i].astype(BF16)
        ws = (g_ssd[i][:, None] * w_o_ssd[i]).astype(BF16)
        w1, w2 = w_ff1[i].astype(BF16), w_ff2[i].astype(BF16)
        g1, g2 = g_norm1[i][None], g_norm2[i][None]

        ctx_lo = 0 if ctx_out else P_K
        hc, dt_c = norm_modulate(h_ctx, g1, mod, ctx_row, 0, w_dt_t)
        p_c, u_c, ssq_c, fin_c = _mixer_side(hc, dt_c, ctx_lo, w_main, conv_w[i], conv_b[i][None],
                                             ssd_params, None, dskip if ctx_out else None)
        hx, dt_x = norm_modulate(x, g1, mod, latent_row, 0, w_dt_t)
        p_x, u_x, ssq_x, _ = _mixer_side(hx, dt_x, 0, w_main, conv_w[i], conv_b[i][None],
                                         ssd_params, fin_c, dskip)
        att_x = latent_attention(p_x, p_c, -ctx_lo, sink, q_tables, k_tables)
        x = _merge_out(att_x, u_x, ssq_x, p_x, wa, ws, wo, x, mod, latent_row)
        x = _mlp(x, g2, w1, w2, mod, latent_row)
        if ctx_out:
            att_c = context_attention(p_c, sink)
            h_ctx = _merge_out(att_c, u_c, ssq_c, p_c, wa, ws, wo, h_ctx, mod, ctx_row)
            h_ctx = _mlp(h_ctx, g2, w1, w2, mod, ctx_row)
    return final_norm(x, g_final[None])
```

```python
import functools

import jax
import jax.numpy as jnp
from jax import lax
from jax.experimental import pallas as pl
from jax.experimental.pallas import tpu as pltpu

F32 = jnp.float32
BF16 = jnp.bfloat16

D_MODEL = 2048
GRID_W = 64
N_HEADS = 16
N_KV_HEADS = 4
HEAD_DIM = 128
REP = N_HEADS // N_KV_HEADS
ATT_WIDTH = N_HEADS * HEAD_DIM
KV_WIDTH = N_KV_HEADS * HEAD_DIM
WINDOW = 128
ATT_BLOCK = 128
ROPE_THETA = 10000.0
D_INNER = 2 * D_MODEL
SSD_HEAD_DIM = 64
SSD_HEADS = D_INNER // SSD_HEAD_DIM
SSD_GROUPS = 8
HEADS_PER_GROUP = SSD_HEADS // SSD_GROUPS
D_STATE = 128
CONV_K = 5
SSD_CHUNK = 128
GROUP_WIDTH = HEADS_PER_GROUP * SSD_HEAD_DIM
CONV_CH = D_INNER + 2 * SSD_GROUPS * D_STATE
D_FF = 4 * D_MODEL
N_MOD = 6
EPS = 1e-6
LOG2E = 1.4426950408889634
NEG_BIG = -1e30
LOG2_DT_FLOOR = -1e4

REF_COL_XBC = 2 * KV_WIDTH
REF_COL_DT = REF_COL_XBC + CONV_CH
REF_COL_Q = REF_COL_DT + 2 * SSD_HEADS
REF_COL_Z = REF_COL_Q + ATT_WIDTH
REF_COL_GATE = REF_COL_Z + D_INNER

P_Z = 0
P_GATE_A = P_Z + D_INNER
P_GATE_S = P_GATE_A + D_MODEL
P_Q = P_GATE_S + D_MODEL
P_K = P_Q + ATT_WIDTH
P_V = P_K + KV_WIDTH
P_XBC = P_V + KV_WIDTH
P_BM = P_XBC + D_INNER
P_CM = P_BM + SSD_GROUPS * D_STATE
P_COLS = P_CM + SSD_GROUPS * D_STATE

LANES = 128
VMEM_LIMIT = 56 * 1024 * 1024


def _params(*sem):
    return pltpu.CompilerParams(dimension_semantics=sem, vmem_limit_bytes=VMEM_LIMIT)


def _mm_kernel(*refs, n_pairs, n_extra, epilogue):
    a_refs = refs[:n_pairs]
    w_refs = refs[n_pairs:2 * n_pairs]
    extra = refs[2 * n_pairs:2 * n_pairs + n_extra]
    o_ref = refs[2 * n_pairs + n_extra]
    parts = [jnp.dot(a[...], w[...], preferred_element_type=F32) for a, w in zip(a_refs, w_refs)]
    o_ref[...] = epilogue(parts, [e[...] for e in extra]).astype(o_ref.dtype)


def _weight_array(w):
    return w[0] if isinstance(w, tuple) else w


def _weight_spec(w, tn, index_map, at=lambda im: im, tk=None):
    if not isinstance(w, tuple):
        return pl.BlockSpec((tk or w.shape[0], tn), at(index_map))
    stack, layer = w
    return pl.BlockSpec((None, tk or stack.shape[1], tn),
                        at(lambda *ij: (layer,) + tuple(index_map(*ij))))


def matmul(pairs, extras, epilogue, n_out, out_dtype, tm, tn, w_col0=0, w_stationary=False,
           name="mm"):
    m = pairs[0][0].shape[0]
    tm = min(tm, m)
    assert m % tm == 0 and n_out % tn == 0 and w_col0 % tn == 0
    jo = w_col0 // tn
    if w_stationary:
        grid = (n_out // tn, m // tm)
        at = lambda im: (lambda j, i: im(i, j))
    else:
        grid = (m // tm, n_out // tn)
        at = lambda im: im
    in_specs = [pl.BlockSpec((tm, a.shape[1]), at(lambda i, j: (i, 0))) for a, _ in pairs]
    in_specs += [_weight_spec(w, tn, lambda i, j: (0, j + jo), at) for _, w in pairs]
    in_specs += [pl.BlockSpec(bs, at(im)) for _, bs, im in extras]
    pairs = [(a, _weight_array(w)) for a, w in pairs]
    kern = functools.partial(_mm_kernel, n_pairs=len(pairs), n_extra=len(extras), epilogue=epilogue)
    return pl.pallas_call(
        kern,
        grid=grid,
        in_specs=in_specs,
        out_specs=pl.BlockSpec((tm, tn), at(lambda i, j: (i, j))),
        out_shape=jax.ShapeDtypeStruct((m, n_out), out_dtype),
        compiler_params=_params("parallel", "arbitrary"),
        name=name,
    )(*[a for a, _ in pairs], *[w for _, w in pairs], *[e for e, _, _ in extras])


def _mm_acc_kernel(*refs, nk, n_extra, epilogue):
    a_ref, w_ref = refs[0], refs[1]
    extra = refs[2:2 + n_extra]
    o_ref, acc_ref = refs[2 + n_extra], refs[3 + n_extra]
    k = pl.program_id(2)

    @pl.when(k == 0)
    def _():
        acc_ref[...] = jnp.zeros_like(acc_ref)

    acc_ref[...] += jnp.dot(a_ref[...], w_ref[...], preferred_element_type=F32)

    @pl.when(k == nk - 1)
    def _():
        o_ref[...] = epilogue([acc_ref[...]], [e[...] for e in extra]).astype(o_ref.dtype)


def matmul_ktiled(a, w, extras, epilogue, out_dtype, tm, tn, tk, name="mmk"):
    m, kdim = a.shape
    n_out = _weight_array(w).shape[-1]
    tm = min(tm, m)
    assert m % tm == 0 and n_out % tn == 0 and kdim % tk == 0
    nk = kdim // tk
    in_specs = [pl.BlockSpec((tm, tk), lambda i, j, k: (i, k)),
                _weight_spec(w, tn, lambda i, j, k: (k, j), tk=tk)]
    in_specs += [pl.BlockSpec(bs, (lambda i, j, k, im=im: im(i, j))) for _, bs, im in extras]
    kern = functools.partial(_mm_acc_kernel, nk=nk, n_extra=len(extras), epilogue=epilogue)
    return pl.pallas_call(
        kern,
        grid=(m // tm, n_out // tn, nk),
        in_specs=in_specs,
        out_specs=pl.BlockSpec((tm, tn), lambda i, j, k: (i, j)),
        out_shape=jax.ShapeDtypeStruct((m, n_out), out_dtype),
        scratch_shapes=[pltpu.VMEM((tm, tn), F32)],
        compiler_params=_params("parallel", "arbitrary", "arbitrary"),
        name=name,
    )(a, _weight_array(w), *[e for e, _, _ in extras])


def _adaln_kernel(c_ref, w_ref, b_ref, o_ref):
    c = c_ref[...]
    silu_c = c * jax.nn.sigmoid(c)
    o_ref[...] = jnp.dot(silu_c, w_ref[...], preferred_element_type=F32,
                         precision=lax.Precision.HIGHEST) + b_ref[...]


def adaln_modulation(cond, w, b, tn=512):
    r, d = cond.shape
    depth, _, n = w.shape
    return pl.pallas_call(
        _adaln_kernel,
        grid=(depth, n // tn),
        in_specs=[pl.BlockSpec((r, d), lambda l, j: (0, 0)),
                  pl.BlockSpec((None, d, tn), lambda l, j: (l, 0, j)),
                  pl.BlockSpec((None, 1, tn), lambda l, j: (l, 0, j))],
        out_specs=pl.BlockSpec((None, r, tn), lambda l, j: (l, 0, j)),
        out_shape=jax.ShapeDtypeStruct((depth, r, n), F32),
        compiler_params=_params("parallel", "arbitrary"),
        name="adaln",
    )(cond, w, b)


def _norm_mod_kernel(x_ref, g_ref, shift_ref, scale_ref, *rest):
    x = x_ref[...]
    y = x * lax.rsqrt(jnp.mean(x * x, axis=-1, keepdims=True) + EPS) * g_ref[...]
    h = (y * (1.0 + scale_ref[...]) + shift_ref[...]).astype(BF16)
    if len(rest) == 1:
        rest[0][...] = h
        return
    wt_ref, o_ref, dt_ref = rest
    o_ref[...] = h
    dt_ref[...] = lax.dot_general(wt_ref[...], h, (((1,), (1,)), ((), ())),
                                  preferred_element_type=F32)


def norm_modulate(x, g, mod, mod_row, shift_idx, w_dt_t=None, ts=1024):
    b, t, d = x.shape
    ts = min(ts, t)
    per_seq = t // ts
    in_specs = [pl.BlockSpec((None, ts, d), lambda i, j: (i, j, 0)),
                pl.BlockSpec((1, d), lambda i, j: (0, 0)),
                pl.BlockSpec((None, 1, d), lambda i, j: (mod_row(i), 0, shift_idx)),
                pl.BlockSpec((None, 1, d), lambda i, j: (mod_row(i), 0, shift_idx + 1))]
    out_specs = pl.BlockSpec((None, ts, d), lambda i, j: (i, j, 0))
    out_shape = jax.ShapeDtypeStruct((b, t, d), BF16)
    args = [x, g, mod, mod]
    if w_dt_t is not None:
        in_specs.append(pl.BlockSpec((LANES, d), lambda i, j: (0, 0)))
        out_specs = [out_specs, pl.BlockSpec((LANES, ts), lambda i, j: (0, i * per_seq + j))]
        out_shape = [out_shape, jax.ShapeDtypeStruct((LANES, b * t), F32)]
        args.append(w_dt_t)
    return pl.pallas_call(
        _norm_mod_kernel,
        grid=(b, t // ts),
        in_specs=in_specs,
        out_specs=out_specs,
        out_shape=out_shape,
        compiler_params=_params("parallel", "parallel"),
        name="norm_mod",
    )(*args)


def _final_norm_kernel(x_ref, g_ref, o_ref):
    x = x_ref[...]
    o_ref[...] = x * lax.rsqrt(jnp.mean(x * x, axis=-1, keepdims=True) + EPS) * g_ref[...]


def final_norm(x, g, ts=1024):
    b, t, d = x.shape
    ts = min(ts, t)
    return pl.pallas_call(
        _final_norm_kernel,
        grid=(b, t // ts),
        in_specs=[pl.BlockSpec((None, ts, d), lambda i, j: (i, j, 0)),
                  pl.BlockSpec((1, d), lambda i, j: (0, 0))],
        out_specs=pl.BlockSpec((None, ts, d), lambda i, j: (i, j, 0)),
        out_shape=jax.ShapeDtypeStruct((b, t, d), F32),
        compiler_params=_params("parallel", "parallel"),
        name="final_norm",
    )(x, g)


CONV_ROWS = 128
CONV_HALO = 16
CONV_UNROLL = 16


def _conv_silu_kernel(u_ref, w_ref, b_ref, o_ref, ext_ref, *, unroll):
    t, tc = u_ref.shape
    pad = CONV_K // 2
    span = CONV_ROWS + 2 * CONV_HALO
    halo = jnp.zeros((CONV_HALO, tc), BF16)
    ext_ref[0:CONV_HALO, :] = halo
    ext_ref[CONV_HALO + t:, :] = halo
    ext_ref[CONV_HALO:CONV_HALO + t, :] = u_ref[...]
    taps = [k - pad for k in range(CONV_K) if k != pad]
    ri = lax.broadcasted_iota(jnp.int32, (len(taps) * CONV_ROWS, span), 0)
    ci = lax.broadcasted_iota(jnp.int32, (len(taps) * CONV_ROWS, span), 1)
    tap_off = jnp.zeros_like(ri)
    for n, off in enumerate(taps):
        tap_off = jnp.where(ri // CONV_ROWS == n, off, tap_off)
    shift_mat = (ci == (ri % CONV_ROWS) + CONV_HALO + tap_off).astype(BF16)
    bias = b_ref[...]
    w = [w_ref[k:k + 1, :] for k in range(CONV_K)]

    def body(it, carry):
        windows, shifted = [], []
        for n in range(unroll):
            r0 = pl.multiple_of((it * unroll + n) * CONV_ROWS, CONV_ROWS)
            win = ext_ref[pl.ds(r0, span), :]
            windows.append((r0, win))
            shifted.append(jnp.dot(shift_mat, win, preferred_element_type=F32))
        for (r0, win), sh in zip(windows, shifted):
            acc = bias + w[pad] * win[CONV_HALO:CONV_HALO + CONV_ROWS].astype(F32)
            for n, off in enumerate(taps):
                acc = acc + w[pad + off] * sh[n * CONV_ROWS:(n + 1) * CONV_ROWS]
            o_ref[pl.ds(r0, CONV_ROWS), :] = (acc * jax.nn.sigmoid(acc)).astype(o_ref.dtype)
        return carry

    lax.fori_loop(0, t // (CONV_ROWS * unroll), body, 0)


def conv_silu(p, col0, conv_w, conv_b):
    b, t, _ = p.shape
    tc = 512 if t >= 1024 else 1024
    unroll = min(CONV_UNROLL, t // CONV_ROWS)
    assert t % (CONV_ROWS * unroll) == 0
    c0 = (col0 + P_XBC) // tc
    return pl.pallas_call(
        functools.partial(_conv_silu_kernel, unroll=unroll),
        grid=(b, CONV_CH // tc),
        in_specs=[pl.BlockSpec((None, t, tc), lambda i, j: (i, 0, c0 + j)),
                  pl.BlockSpec((CONV_K, tc), lambda i, j: (0, j)),
                  pl.BlockSpec((1, tc), lambda i, j: (0, j))],
        out_specs=pl.BlockSpec((None, t, tc), lambda i, j: (i, 0, j)),
        out_shape=jax.ShapeDtypeStruct((b, t, CONV_CH), BF16),
        scratch_shapes=[pltpu.VMEM((t + 2 * CONV_HALO, tc), BF16)],
        compiler_params=_params("parallel", "parallel"),
        name="conv_silu",
    )(p, conv_w, conv_b)


def _rope(u, cos, sin_next, sin_prev):
    quarter = HEAD_DIM // 4
    return (u * cos + pltpu.roll(u, HEAD_DIM - quarter, axis=1) * sin_next
            + pltpu.roll(u, quarter, axis=1) * sin_prev)


def _softmax_pv(s, sink_col, v):
    m = jnp.maximum(jnp.max(s, axis=-1, keepdims=True), sink_col)
    e = jnp.exp(s - m)
    denom = jnp.sum(e, axis=-1, keepdims=True) + jnp.exp(sink_col - m)
    o = jnp.dot(e.astype(BF16), v, preferred_element_type=F32)
    return o / denom


def _sink_column(sink_ref, h, rows):
    blk = lax.broadcasted_iota(jnp.int32, (REP * rows, 1), 0) // rows
    col = jnp.zeros((REP * rows, 1), F32)
    for r in range(REP):
        col = jnp.where(blk == r, sink_ref[h * REP + r], col)
    return col


def _latent_attn_kernel(sink_ref, q_ref, kp_ref, kc_ref, kn_ref, vp_ref, vc_ref, vn_ref,
                        kx_ref, vx_ref, qcos_ref, qsn_ref, qsp_ref, cos_ref, sn_ref, sp_ref, o_ref,
                        *, nb):
    n = pl.program_id(1)
    blk = ATT_BLOCK

    def rows(ref, i):
        return ref[pl.ds(pl.multiple_of(i * blk, blk), blk), :]

    n_prev = jnp.maximum(n - 1, 0)
    n_next = jnp.minimum(n + 1, nb - 1)
    k_tables = [[rows(ref, i) for ref in (cos_ref, sn_ref, sp_ref)] for i in (n_prev, n, n_next)]
    qcos, qsn, qsp = rows(qcos_ref, n), rows(qsn_ref, n), rows(qsp_ref, n)
    qi = lax.broadcasted_iota(jnp.int32, (blk, blk), 0)
    kj = lax.broadcasted_iota(jnp.int32, (blk, blk), 1)
    see_prev = kj >= qi + (n == 0).astype(jnp.int32) * blk
    see_next = kj <= qi - (n == nb - 1).astype(jnp.int32) * blk
    def head_cols(h):
        return [slice((h * REP + r) * HEAD_DIM, (h * REP + r + 1) * HEAD_DIM) for r in range(REP)]

    def scores(h):
        kv = slice(h * HEAD_DIM, (h + 1) * HEAD_DIM)
        keys = jnp.concatenate(
            [_rope(ref[:, kv].astype(F32), *tabs).astype(BF16)
             for ref, tabs in zip((kp_ref, kc_ref, kn_ref), k_tables)] + [kx_ref[:, kv]], axis=0)
        return [lax.dot_general(_rope(q_ref[:, cols].astype(F32), qcos, qsn, qsp).astype(BF16),
                                keys, (((1,), (1,)), ((), ())), preferred_element_type=F32)
                for cols in head_cols(h)]

    def softmax_terms(h, logits):
        probs, sink_terms = [], []
        for r, s in enumerate(logits):
            s = jnp.concatenate(
                [jnp.where(see_prev, s[:, :blk], -jnp.inf), s[:, blk:2 * blk],
                 jnp.where(see_next, s[:, 2 * blk:3 * blk], -jnp.inf), s[:, 3 * blk:]], axis=1)
            sink = sink_ref[h * REP + r] * LOG2E
            m = jnp.maximum(jnp.max(s, axis=-1, keepdims=True), sink)
            probs.append(jnp.exp2(s - m).astype(BF16))
            sink_terms.append(jnp.exp2(sink - m))
        return probs, sink_terms

    def outputs(h, probs, sink_terms):
        kv = slice(h * HEAD_DIM, (h + 1) * HEAD_DIM)
        vals = jnp.concatenate([vp_ref[:, kv], vc_ref[:, kv], vn_ref[:, kv], vx_ref[:, kv]], axis=0)
        vals = jnp.concatenate([vals, jnp.ones_like(vals)], axis=1)
        for cols, e, sink_term in zip(head_cols(h), probs, sink_terms):
            o = jnp.dot(e, vals, preferred_element_type=F32)
            o_ref[:, cols] = (o[:, :HEAD_DIM] / (o[:, HEAD_DIM:] + sink_term)).astype(o_ref.dtype)

    logits = scores(0)
    for h in range(N_KV_HEADS):
        ahead = scores(h + 1) if h + 1 < N_KV_HEADS else None
        outputs(h, *softmax_terms(h, logits))
        logits = ahead


def latent_attention(p, p_ctx, ctx_col0, sink, q_tables, k_tables):
    b, s, _ = p.shape
    l = p_ctx.shape[1]
    nb = s // ATT_BLOCK
    blk = ATT_BLOCK
    kb, vb = P_K // KV_WIDTH, P_V // KV_WIDTH
    kxb, vxb = (ctx_col0 + P_K) // KV_WIDTH, (ctx_col0 + P_V) // KV_WIDTH
    prev = lambda n: jnp.maximum(n - 1, 0)
    nxt = lambda n: jnp.minimum(n + 1, nb - 1)
    kv_spec = lambda cb, f: pl.BlockSpec((None, blk, KV_WIDTH), lambda i, n: (i, f(n), cb))
    table = pl.BlockSpec((s, HEAD_DIM), lambda i, n: (0, 0))
    return pl.pallas_call(
        functools.partial(_latent_attn_kernel, nb=nb),
        grid=(b, nb),
        in_specs=[pl.BlockSpec(memory_space=pltpu.SMEM),
                  pl.BlockSpec((None, blk, ATT_WIDTH), lambda i, n: (i, n, P_Q // ATT_WIDTH)),
                  kv_spec(kb, prev), kv_spec(kb, lambda n: n), kv_spec(kb, nxt),
                  kv_spec(vb, prev), kv_spec(vb, lambda n: n), kv_spec(vb, nxt),
                  pl.BlockSpec((None, l, KV_WIDTH), lambda i, n: (i, 0, kxb)),
                  pl.BlockSpec((None, l, KV_WIDTH), lambda i, n: (i, 0, vxb)),
                  table, table, table, table, table, table],
        out_specs=pl.BlockSpec((None, blk, ATT_WIDTH), lambda i, n: (i, n, 0)),
        out_shape=jax.ShapeDtypeStruct((b, s, ATT_WIDTH), BF16),
        compiler_params=_params("parallel", "parallel"),
        name="latent_attn",
    )(sink, p, p, p, p, p, p, p, p_ctx, p_ctx, *q_tables, *k_tables)


def _context_attn_kernel(sink_ref, q_ref, k_ref, v_ref, o_ref):
    h = pl.program_id(2)
    blk = q_ref.shape[0]
    q = jnp.concatenate([q_ref[:, r * HEAD_DIM:(r + 1) * HEAD_DIM] for r in range(REP)], axis=0)
    s = lax.dot_general(q, k_ref[...], (((1,), (1,)), ((), ())), preferred_element_type=F32)
    o = _softmax_pv(s * HEAD_DIM ** -0.5, _sink_column(sink_ref, h, blk), v_ref[...])
    for r in range(REP):
        o_ref[:, r * HEAD_DIM:(r + 1) * HEAD_DIM] = o[r * blk:(r + 1) * blk].astype(o_ref.dtype)


def context_attention(p_ctx, sink):
    b, l, _ = p_ctx.shape
    blk = min(ATT_BLOCK, l)
    kb, vb = P_K // HEAD_DIM, P_V // HEAD_DIM
    qb = P_Q // (REP * HEAD_DIM)
    return pl.pallas_call(
        _context_attn_kernel,
        grid=(b, l // blk, N_KV_HEADS),
        in_specs=[pl.BlockSpec(memory_space=pltpu.SMEM),
                  pl.BlockSpec((None, blk, REP * HEAD_DIM), lambda i, n, h: (i, n, qb + h)),
                  pl.BlockSpec((None, l, HEAD_DIM), lambda i, n, h: (i, 0, kb + h)),
                  pl.BlockSpec((None, l, HEAD_DIM), lambda i, n, h: (i, 0, vb + h))],
        out_specs=pl.BlockSpec((None, blk, REP * HEAD_DIM), lambda i, n, h: (i, n, h)),
        out_shape=jax.ShapeDtypeStruct((b, l, ATT_WIDTH), BF16),
        compiler_params=_params("parallel", "parallel", "arbitrary"),
        name="context_attn",
    )(sink, p_ctx, p_ctx, p_ctx)


def _split3(x):
    hi = x.astype(BF16).astype(F32)
    r1 = x - hi
    mid = r1.astype(BF16).astype(F32)
    lo = (r1 - mid).astype(BF16).astype(F32)
    return hi, mid, lo


def _softplus(x):
    return jnp.maximum(x, 0.0) + jnp.log1p(jnp.exp(-jnp.abs(x)))


def _ssd_kernel(*refs, nc, zero_init, with_output):
    refs = list(refs)
    x_ref, b_ref, c_ref, dtt_ref, bias_ref, alog_ref = refs[:6]
    del refs[:6]
    z_ref, dskip_ref = (refs.pop(0), refs.pop(0)) if with_output else (None, None)
    init_ref = None if zero_init else refs.pop(0)
    u_ref, ssq_ref = (refs.pop(0), refs.pop(0)) if with_output else (None, None)
    fin_ref, state_ref, rhs_ref, decay_ref, exit_ref, entry_ref = refs[:6]
    y_ref = refs[6] if with_output else None
    q = SSD_CHUNK
    e_heads = HEADS_PER_GROUP
    wide = e_heads * q
    ii = lax.broadcasted_iota(jnp.int32, (q, q), 0)
    jj = lax.broadcasted_iota(jnp.int32, (q, q), 1)
    eye = (ii == jj).astype(BF16)
    left_b = (jj < SSD_HEAD_DIM).astype(BF16)
    right_b = (jj >= SSD_HEAD_DIM).astype(BF16)
    r32 = lax.broadcasted_iota(jnp.int32, (32, wide), 0)
    c32 = lax.broadcasted_iota(jnp.int32, (32, wide), 1) // q
    head_rows = ((r32 < 3 * e_heads) & ((r32 & (e_heads - 1)) == c32)).astype(F32)
    r8 = lax.broadcasted_iota(jnp.int32, (e_heads, wide), 0)
    c8 = lax.broadcasted_iota(jnp.int32, (e_heads, wide), 1) // q
    own_block = (r8 == c8).astype(F32)
    re = lax.broadcasted_iota(jnp.int32, (q, GROUP_WIDTH), 0)
    ce = lax.broadcasted_iota(jnp.int32, (q, GROUP_WIDTH), 1) // SSD_HEAD_DIM
    expand = ((re >= 64) & (re < 64 + 3 * e_heads) & ((re & (e_heads - 1)) == ce)).astype(BF16)
    zeros8 = jnp.zeros((e_heads, q), F32)
    ones24 = jnp.ones((3 * e_heads, q), F32)

    head_params = []
    for d in range(2):
        lo, hi = d * e_heads, (d + 1) * e_heads
        visible = (jj >= ii) if d else (jj <= ii)
        cum_mat = ((jj <= ii) if d else (jj >= ii)).astype(BF16)
        head_params.append((bias_ref[lo:hi, :], -jnp.exp(alog_ref[lo:hi, :]), cum_mat))

        @pl.when(pl.program_id(1) == 0)
        def _(d=d, visible=visible):
            rhs_ref[d, 0:q, :] = jnp.tile(jnp.where(visible, 0.0, NEG_BIG),
                                          (1, e_heads)).astype(BF16)
            rhs_ref[d, q:q + 32, :] = head_rows.astype(BF16)
            rhs_ref[d, q + 32:2 * q, :] = jnp.zeros((q - 32, wide), BF16)

        if zero_init:
            state_ref[d] = jnp.zeros(state_ref.shape[1:], F32)
        else:
            state_ref[d] = init_ref[d]

    def chunk_rows(d, step):
        c = (nc - 1 - step) if d else step
        return pl.ds(c * q if isinstance(c, int) else pl.multiple_of(c * q, q), q)

    def prepare_cumsum(d, step):
        lo, hi = d * e_heads, (d + 1) * e_heads
        bias_t, aneg_t, cum_mat = head_params[d]
        rows = chunk_rows(d, step)
        dt = _softplus(dtt_ref[lo:hi, rows] + bias_t)
        da = jnp.concatenate(_split3(dt * aneg_t), axis=0).astype(BF16)
        return dict(d=d, rows=rows, dt=dt, cs=jnp.dot(da, cum_mat, preferred_element_type=F32))

    def prepare_matmuls(v):
        d, dt, cs = v["d"], v["dt"], v["cs"]
        last = 0 if d else q - 1
        bc = b_ref[v["rows"], :]
        cc = c_ref[v["rows"], :]
        a2 = (cs[0:e_heads] + cs[e_heads:2 * e_heads] + cs[2 * e_heads:]) * LOG2E
        ea = jnp.exp2(a2)
        w_exit = jnp.exp2(a2[:, last:last + 1] - a2) * dt
        g = jnp.maximum(jnp.log2(dt), LOG2_DT_FLOOR) - a2
        rhs_ref[d, q + 32:q + 64, :] = jnp.concatenate(
            [jnp.tile(part, (1, e_heads)) * own_block for part in _split3(g)]
            + [jnp.zeros((e_heads, wide), F32)], axis=0).astype(BF16)
        split_t = jnp.concatenate(list(_split3(a2)) + [zeros8, ones24, zeros8]
                                  + list(_split3(ea)) + [jnp.zeros((q - 88, q), F32)], axis=0)
        split_n = split_t.T.astype(BF16)
        v["seg"] = jnp.dot(jnp.concatenate([eye, split_n], axis=1), rhs_ref[d],
                           preferred_element_type=F32)
        v["ea_wide"] = jnp.dot(split_n, expand, preferred_element_type=F32)
        v["cb"] = lax.dot_general(cc, bc, (((1,), (1,)), ((), ())), preferred_element_type=F32)
        b_t = bc.astype(F32).T
        v["b_exit"] = jnp.concatenate([b_t * w_exit[e:e + 1, :] for e in range(e_heads)],
                                      axis=1).astype(BF16)

    def prepare_store(v, slot):
        d = v["d"]
        decay_ref[slot, d] = (jnp.tile(v["cb"], (1, e_heads)) * jnp.exp2(v["seg"])).astype(BF16)
        exit_ref[slot, d] = v["b_exit"]
        entry_ref[slot, d] = v["ea_wide"]

    def emit_entry(d, step):
        rows = chunk_rows(d, step)
        state = state_ref[d]
        w = dict(d=d, rows=rows, state=state)
        if with_output:
            w["y_off"] = jnp.dot(c_ref[rows, :], state.astype(BF16), preferred_element_type=F32)
        return w

    def emit_outputs(w, slot, second_visit):
        d, rows = w["d"], w["rows"]
        last = 0 if d else q - 1
        xc = x_ref[rows, :]
        ssq = jnp.zeros((q, LANES), F32)
        for p in range(e_heads // 2):
            cols = slice(p * LANES, (p + 1) * LANES)
            two = slice(2 * p * q, 2 * (p + 1) * q)
            xp = xc[:, cols]
            x_pair = jnp.concatenate([xp * left_b, xp * right_b], axis=0)
            entry = entry_ref[slot, d, :, cols]
            if not with_output:
                new = jnp.dot(exit_ref[slot, d, :, two], x_pair, preferred_element_type=F32)
                state_ref[d, :, cols] = w["state"][:, cols] * entry[last:last + 1, :] + new
                continue
            lhs = jnp.concatenate([decay_ref[slot, d, :, two], exit_ref[slot, d, :, two]], axis=0)
            out = jnp.dot(lhs, x_pair, preferred_element_type=F32)
            state_ref[d, :, cols] = w["state"][:, cols] * entry[last:last + 1, :] + out[q:]
            y = out[:q] + w["y_off"][:, cols] * entry
            if not second_visit:
                y_ref[rows, cols] = y
                continue
            z = z_ref[rows, cols].astype(F32)
            y = y_ref[rows, cols] + y + xp.astype(F32) * dskip_ref[:, cols]
            gated = y * (z * jax.nn.sigmoid(z))
            u_ref[rows, cols] = gated.astype(u_ref.dtype)
            ssq = ssq + gated * gated
        if with_output and second_visit:
            ssq_ref[rows, :] = ssq

    def prepare_all(step, slot):
        chains = [prepare_cumsum(d, step) for d in range(2)]
        for v in chains:
            prepare_matmuls(v)
        for v in chains:
            prepare_store(v, slot)

    def body(second_visits, final=False):
        def run(it, carry):
            for half in range(2):
                step = 2 * it + half
                look_ahead = not (final and half)
                chains = [prepare_cumsum(d, step + 1) for d in range(2)] if look_ahead else []
                entries = [emit_entry(d, step) for d in range(2)]
                for n, w in enumerate(entries):
                    emit_outputs(w, half, second_visits[half])
                    if look_ahead:
                        prepare_matmuls(chains[n])
                for v in chains:
                    prepare_store(v, 1 - half)
            return carry
        return run

    prepare_all(0, 0)
    n_it = nc // 2
    if nc == 2:
        body((False, True), final=True)(0, 0)
    else:
        lax.fori_loop(0, n_it // 2, body((False, False)), 0)
        lax.fori_loop(n_it // 2, n_it - 1, body((True, True)), 0)
        body((True, True), final=True)(n_it - 1, 0)
    for d in range(2):
        fin_ref[d] = state_ref[d]


def ssd_scan(xbc, dt_t, bias_t, alog_t, init, gate=None):
    b, t, _ = xbc.shape
    nc = t // SSD_CHUNK
    assert nc == 2 or nc % 4 == 0
    wide = HEADS_PER_GROUP * SSD_CHUNK
    g = SSD_GROUPS
    e2 = 2 * HEADS_PER_GROUP
    bm0 = D_INNER // D_STATE
    cm0 = bm0 + g
    state_shape = (2, D_STATE, GROUP_WIDTH)
    group_block = pl.BlockSpec((None, t, GROUP_WIDTH), lambda i, j: (i, 0, j))
    state_block = pl.BlockSpec((None, None) + state_shape, lambda i, j: (i, j, 0, 0, 0))
    in_specs = [group_block,
                pl.BlockSpec((None, t, D_STATE), lambda i, j: (i, 0, bm0 + j)),
                pl.BlockSpec((None, t, D_STATE), lambda i, j: (i, 0, cm0 + j)),
                pl.BlockSpec((e2, t), lambda i, j: (j, i)),
                pl.BlockSpec((None, e2, 1), lambda i, j: (j, 0, 0)),
                pl.BlockSpec((None, e2, 1), lambda i, j: (j, 0, 0))]
    args = [xbc, xbc, xbc, dt_t, bias_t, alog_t]
    out_specs, out_shape = [state_block], [jax.ShapeDtypeStruct((b, g) + state_shape, F32)]
    scratch = [pltpu.VMEM(state_shape, F32),
               pltpu.VMEM((2, 2 * SSD_CHUNK, wide), BF16),
               pltpu.VMEM((2, 2, SSD_CHUNK, wide), BF16),
               pltpu.VMEM((2, 2, D_STATE, wide), BF16),
               pltpu.VMEM((2, 2, SSD_CHUNK, GROUP_WIDTH), F32)]
    if gate is not None:
        p, dskip = gate
        z0 = P_Z // GROUP_WIDTH
        in_specs += [pl.BlockSpec((None, t, GROUP_WIDTH), lambda i, j: (i, 0, z0 + j)),
                     pl.BlockSpec((None, 1, GROUP_WIDTH), lambda i, j: (j, 0, 0))]
        args += [p, dskip]
        out_specs = [group_block, pl.BlockSpec((None, t, LANES), lambda i, j: (i, 0, j))] + out_specs
        out_shape = [jax.ShapeDtypeStruct((b, t, D_INNER), BF16),
                     jax.ShapeDtypeStruct((b, t, g * LANES), F32)] + out_shape
        scratch.append(pltpu.VMEM((t, GROUP_WIDTH), F32))
    if init is not None:
        in_specs.append(state_block)
        args.append(init)
    outs = pl.pallas_call(
        functools.partial(_ssd_kernel, nc=nc, zero_init=init is None, with_output=gate is not None),
        grid=(b, g),
        in_specs=in_specs,
        out_specs=out_specs,
        out_shape=out_shape,
        scratch_shapes=scratch,
        compiler_params=_params("parallel", "arbitrary"),
        name="ssd_scan",
    )(*args)
    return tuple(outs) if gate is not None else (None, None, outs[0])


def _rope_tables(seq):
    t = jnp.arange(seq)
    row = (t // GRID_W).astype(F32)
    col = (t % GRID_W).astype(F32)
    axis_dim = HEAD_DIM // 2
    inv_freq = ROPE_THETA ** (-jnp.arange(0, axis_dim, 2, dtype=F32) / axis_dim)
    ang_r = row[:, None] * inv_freq[None]
    ang_c = col[:, None] * inv_freq[None]
    cos_r, sin_r, cos_c, sin_c = jnp.cos(ang_r), jnp.sin(ang_r), jnp.cos(ang_c), jnp.sin(ang_c)
    zero = jnp.zeros_like(sin_r)
    cos = jnp.concatenate([cos_r, cos_r, cos_c, cos_c], axis=-1)
    sin_next = jnp.concatenate([-sin_r, zero, -sin_c, zero], axis=-1)
    sin_prev = jnp.concatenate([zero, sin_r, zero, sin_c], axis=-1)
    return cos, sin_next, sin_prev


def _group_major(v):
    return v.reshape(2, SSD_GROUPS, HEADS_PER_GROUP).transpose(1, 0, 2).reshape(
        SSD_GROUPS, 2 * HEADS_PER_GROUP)


def _mixer_side(h, dt_t, p_cols_lo, w_main, conv_w, conv_b, ssd_params, init, dskip):
    b, t, d = h.shape
    h2 = h.reshape(b * t, d)
    n_out = P_COLS - p_cols_lo
    p = matmul([(h2, w_main)], [], lambda parts, ex: parts[0], n_out, BF16, tm=1024, tn=1024,
               w_col0=p_cols_lo, name="in_proj").reshape(b, t, n_out)
    xbc = conv_silu(p, -p_cols_lo, conv_w, conv_b)
    gate = None if dskip is None else (p, dskip)
    u, ssq, fin = ssd_scan(xbc, dt_t, *ssd_params, init, gate)
    return p, u, ssq, fin


def _merge_out(att, u, ssq, p, w_o_attn, w_o_ssd_g, w_out, x, mod, mod_row):
    b, t, d = x.shape
    m = b * t
    tm, tn = min(512, t), 1024
    p2 = p.reshape(m, p.shape[-1])
    ga0, gs0 = P_GATE_A // tn, P_GATE_S // tn
    n_part = ssq.shape[-1]

    def merge(parts, ex):
        ga, gs, sq = ex
        inv_rms = lax.rsqrt(jnp.sum(sq, axis=-1, keepdims=True) * (1.0 / D_INNER) + EPS)
        return (jax.nn.sigmoid(ga.astype(F32)) * parts[0]
                + jax.nn.sigmoid(gs.astype(F32)) * (inv_rms * parts[1]))

    merged = matmul([(att.reshape(m, ATT_WIDTH), w_o_attn), (u.reshape(m, D_INNER), w_o_ssd_g)],
                    [(p2, (tm, tn), lambda i, j: (i, ga0 + j)),
                     (p2, (tm, tn), lambda i, j: (i, gs0 + j)),
                     (ssq.reshape(m, n_part), (tm, n_part), lambda i, j: (i, 0))],
                    merge, d, BF16, tm=tm, tn=tn, w_stationary=True, name="merge")
    return _residual_matmul(merged, w_out, x, mod, mod_row, 2, name="out_proj")


def _residual_matmul(a, w, x, mod, mod_row, gate_idx, name):
    b, t, d = x.shape
    ktiled = a.shape[1] > 4096
    tm, tn = (min(1024, t), 1024) if ktiled else (min(512, t), d)
    per_seq = t // tm
    nj = d // tn
    extras = [(x.reshape(b * t, d), (tm, tn), lambda i, j: (i, j)),
              (mod, (None, 1, tn), lambda i, j: (mod_row(i // per_seq), 0, gate_idx * nj + j))]
    res = lambda parts, ex: ex[0] + ex[1] * parts[0]
    if ktiled:
        out = matmul_ktiled(a, w, extras, res, F32, tm=tm, tn=tn, tk=2048, name=name)
    else:
        out = matmul([(a, w)], extras, res, d, F32, tm=tm, tn=tn, name=name)
    return out.reshape(b, t, d)


def _mlp(x, g, w1, w2, mod, mod_row):
    b, t, d = x.shape
    h = norm_modulate(x, g, mod, mod_row, 3).reshape(b * t, d)
    hid = matmul([(h, w1)], [], lambda parts, ex: jnp.square(jnp.maximum(parts[0], 0.0)), D_FF,
                 BF16, tm=1024, tn=1024, name="ff1")
    return _residual_matmul(hid, w2, x, mod, mod_row, 5, name="ff2")


def kernel(x, c, ctx, c_ctx, w_ada, b_ada, g_norm1, g_norm2, w_in, attn_sink, conv_w, conv_b,
           dt_bias, a_log, d_skip, g_ssd, w_o_attn, w_o_ssd, w_out, w_ff1, w_ff2, g_final):
    depth = w_in.shape[0]
    batch, seq, d = x.shape
    q_scale = LOG2E * HEAD_DIM ** -0.5
    k_tables = _rope_tables(seq)
    q_tables = tuple(tb * q_scale for tb in k_tables)

    n_rows = -(-(batch + 1) // 8) * 8
    cond = jnp.zeros((n_rows, d), F32).at[:batch].set(c).at[batch].set(c_ctx)
    mod = adaln_modulation(cond, w_ada, b_ada[:, None, :]).reshape(depth * n_rows, 1, N_MOD * d)
    w_main_all = jnp.concatenate(
        [w_in[:, :, REF_COL_Z:REF_COL_GATE], w_in[:, :, REF_COL_GATE:],
         w_in[:, :, REF_COL_Q:REF_COL_Z], w_in[:, :, :REF_COL_DT]], axis=2).astype(BF16)
    wa_all, wo_all = w_o_attn.astype(BF16), w_out.astype(BF16)
    ws_all = (g_ssd[:, :, None] * w_o_ssd).astype(BF16)
    w1_all, w2_all = w_ff1.astype(BF16), w_ff2.astype(BF16)

    h_ctx = ctx
    for i in range(depth):
        ctx_out = i < depth - 1
        latent_row = lambda b, i=i: i * n_rows + b
        ctx_row = lambda b, i=i: i * n_rows + batch
        w_main, wa, ws, wo = (w_main_all, i), (wa_all, i), (ws_all, i), (wo_all, i)
        w1, w2 = (w1_all, i), (w2_all, i)
        w_dt_t = w_in[i][:, REF_COL_DT:REF_COL_Q].reshape(
            d, 2, SSD_GROUPS, HEADS_PER_GROUP).transpose(2, 1, 3, 0).reshape(
                2 * SSD_HEADS, d).astype(BF16)
        ssd_params = (_group_major(dt_bias[i])[:, :, None], _group_major(a_log[i])[:, :, None])
        dskip = jnp.repeat(d_skip[i][0] + d_skip[i][1], SSD_HEAD_DIM).reshape(
            SSD_GROUPS, 1, GROUP_WIDTH)
        sink = attn_sink[i]
        g1, g2 = g_norm1[i][None], g_norm2[i][None]

        ctx_lo = 0 if ctx_out else P_K
        hc, dt_c = norm_modulate(h_ctx, g1, mod, ctx_row, 0, w_dt_t)
        p_c, u_c, ssq_c, fin_c = _mixer_side(hc, dt_c, ctx_lo, w_main, conv_w[i], conv_b[i][None],
                                             ssd_params, None, dskip if ctx_out else None)
        hx, dt_x = norm_modulate(x, g1, mod, latent_row, 0, w_dt_t)
        p_x, u_x, ssq_x, _ = _mixer_side(hx, dt_x, 0, w_main, conv_w[i], conv_b[i][None],
                                         ssd_params, fin_c, dskip)
        att_x = latent_attention(p_x, p_c, -ctx_lo, sink, q_tables, k_tables)
        x = _merge_out(att_x, u_x, ssq_x, p_x, wa, ws, wo, x, mod, latent_row)
        x = _mlp(x, g2, w1, w2, mod, latent_row)
        if ctx_out:
            att_c = context_attention(p_c, sink)
            h_ctx = _merge_out(att_c, u_c, ssq_c, p_c, wa, ws, wo, h_ctx, mod, ctx_row)
            h_ctx = _mlp(h_ctx, g2, w1, w2, mod, ctx_row)
    return final_norm(x, g_final[None])
```

```python
import functools

import jax
import jax.numpy as jnp
from jax import lax
from jax.experimental import pallas as pl
from jax.experimental.pallas import tpu as pltpu

F32 = jnp.float32
BF16 = jnp.bfloat16

D_MODEL = 2048
GRID_W = 64
N_HEADS = 16
N_KV_HEADS = 4
HEAD_DIM = 128
REP = N_HEADS // N_KV_HEADS
ATT_WIDTH = N_HEADS * HEAD_DIM
KV_WIDTH = N_KV_HEADS * HEAD_DIM
WINDOW = 128
ATT_BLOCK = 128
ROPE_THETA = 10000.0
D_INNER = 2 * D_MODEL
SSD_HEAD_DIM = 64
SSD_HEADS = D_INNER // SSD_HEAD_DIM
SSD_GROUPS = 8
HEADS_PER_GROUP = SSD_HEADS // SSD_GROUPS
D_STATE = 128
CONV_K = 5
SSD_CHUNK = 128
GROUP_WIDTH = HEADS_PER_GROUP * SSD_HEAD_DIM
CONV_CH = D_INNER + 2 * SSD_GROUPS * D_STATE
D_FF = 4 * D_MODEL
N_MOD = 6
EPS = 1e-6
LOG2E = 1.4426950408889634
NEG_BIG = -1e30
LOG2_DT_FLOOR = -1e4

REF_COL_XBC = 2 * KV_WIDTH
REF_COL_DT = REF_COL_XBC + CONV_CH
REF_COL_Q = REF_COL_DT + 2 * SSD_HEADS
REF_COL_Z = REF_COL_Q + ATT_WIDTH
REF_COL_GATE = REF_COL_Z + D_INNER

KVX_V = KV_WIDTH
KVX_XBC = 2 * KV_WIDTH
ZG_GATE_A = D_INNER
ZG_GATE_S = D_INNER + D_MODEL

LANES = 128
VMEM_LIMIT = 56 * 1024 * 1024


def _params(*sem):
    return pltpu.CompilerParams(dimension_semantics=sem, vmem_limit_bytes=VMEM_LIMIT)


def _mm_kernel(*refs, n_pairs, n_extra, epilogue):
    a_refs = refs[:n_pairs]
    w_refs = refs[n_pairs:2 * n_pairs]
    extra = refs[2 * n_pairs:2 * n_pairs + n_extra]
    o_ref = refs[2 * n_pairs + n_extra]
    parts = [jnp.dot(a[...], w[...], preferred_element_type=F32) for a, w in zip(a_refs, w_refs)]
    o_ref[...] = epilogue(parts, [e[...] for e in extra]).astype(o_ref.dtype)


def _weight_array(w):
    return w[0] if isinstance(w, tuple) else w


def _weight_spec(w, tn, index_map, at=lambda im: im, tk=None):
    if not isinstance(w, tuple):
        return pl.BlockSpec((tk or w.shape[0], tn), at(index_map))
    stack, layer = w
    return pl.BlockSpec((None, tk or stack.shape[1], tn),
                        at(lambda *ij: (layer,) + tuple(index_map(*ij))))


def matmul(pairs, extras, epilogue, n_out, out_dtype, tm, tn, w_stationary=False, name="mm"):
    m = pairs[0][0].shape[0]
    tm = min(tm, m)
    assert m % tm == 0 and n_out % tn == 0
    if w_stationary:
        grid = (n_out // tn, m // tm)
        at = lambda im: (lambda j, i: im(i, j))
    else:
        grid = (m // tm, n_out // tn)
        at = lambda im: im
    in_specs = [pl.BlockSpec((tm, a.shape[1]), at(lambda i, j: (i, 0))) for a, _ in pairs]
    in_specs += [_weight_spec(w, tn, lambda i, j: (0, j), at) for _, w in pairs]
    in_specs += [pl.BlockSpec(bs, at(im)) for _, bs, im in extras]
    pairs = [(a, _weight_array(w)) for a, w in pairs]
    kern = functools.partial(_mm_kernel, n_pairs=len(pairs), n_extra=len(extras), epilogue=epilogue)
    return pl.pallas_call(
        kern,
        grid=grid,
        in_specs=in_specs,
        out_specs=pl.BlockSpec((tm, tn), at(lambda i, j: (i, j))),
        out_shape=jax.ShapeDtypeStruct((m, n_out), out_dtype),
        compiler_params=_params("parallel", "arbitrary"),
        name=name,
    )(*[a for a, _ in pairs], *[w for _, w in pairs], *[e for e, _, _ in extras])


def _mm_acc_kernel(*refs, nk, n_extra, epilogue):
    a_ref, w_ref = refs[0], refs[1]
    extra = refs[2:2 + n_extra]
    o_ref, acc_ref = refs[2 + n_extra], refs[3 + n_extra]
    k = pl.program_id(2)

    @pl.when(k == 0)
    def _():
        acc_ref[...] = jnp.zeros_like(acc_ref)

    acc_ref[...] += jnp.dot(a_ref[...], w_ref[...], preferred_element_type=F32)

    @pl.when(k == nk - 1)
    def _():
        o_ref[...] = epilogue([acc_ref[...]], [e[...] for e in extra]).astype(o_ref.dtype)


def matmul_ktiled(a, w, extras, epilogue, out_dtype, tm, tn, tk, name="mmk"):
    m, kdim = a.shape
    n_out = _weight_array(w).shape[-1]
    tm = min(tm, m)
    assert m % tm == 0 and n_out % tn == 0 and kdim % tk == 0
    nk = kdim // tk
    in_specs = [pl.BlockSpec((tm, tk), lambda i, j, k: (i, k)),
                _weight_spec(w, tn, lambda i, j, k: (k, j), tk=tk)]
    in_specs += [pl.BlockSpec(bs, (lambda i, j, k, im=im: im(i, j))) for _, bs, im in extras]
    kern = functools.partial(_mm_acc_kernel, nk=nk, n_extra=len(extras), epilogue=epilogue)
    return pl.pallas_call(
        kern,
        grid=(m // tm, n_out // tn, nk),
        in_specs=in_specs,
        out_specs=pl.BlockSpec((tm, tn), lambda i, j, k: (i, j)),
        out_shape=jax.ShapeDtypeStruct((m, n_out), out_dtype),
        scratch_shapes=[pltpu.VMEM((tm, tn), F32)],
        compiler_params=_params("parallel", "arbitrary", "arbitrary"),
        name=name,
    )(a, _weight_array(w), *[e for e, _, _ in extras])


def _adaln_kernel(c_ref, w_ref, b_ref, o_ref):
    c = c_ref[...]
    silu_c = c * jax.nn.sigmoid(c)
    o_ref[...] = jnp.dot(silu_c, w_ref[...], preferred_element_type=F32,
                         precision=lax.Precision.HIGHEST) + b_ref[...]


def adaln_modulation(cond, w, b, tn=512):
    r, d = cond.shape
    depth, _, n = w.shape
    return pl.pallas_call(
        _adaln_kernel,
        grid=(depth, n // tn),
        in_specs=[pl.BlockSpec((r, d), lambda l, j: (0, 0)),
                  pl.BlockSpec((None, d, tn), lambda l, j: (l, 0, j)),
                  pl.BlockSpec((None, 1, tn), lambda l, j: (l, 0, j))],
        out_specs=pl.BlockSpec((None, r, tn), lambda l, j: (l, 0, j)),
        out_shape=jax.ShapeDtypeStruct((depth, r, n), F32),
        compiler_params=_params("parallel", "arbitrary"),
        name="adaln",
    )(cond, w, b)


def _norm_mod_kernel(x_ref, g_ref, shift_ref, scale_ref, *rest):
    x = x_ref[...]
    y = x * lax.rsqrt(jnp.mean(x * x, axis=-1, keepdims=True) + EPS) * g_ref[...]
    h = (y * (1.0 + scale_ref[...]) + shift_ref[...]).astype(BF16)
    if len(rest) == 1:
        rest[0][...] = h
        return
    wt_ref, o_ref, dt_ref = rest
    o_ref[...] = h
    dt_ref[...] = lax.dot_general(wt_ref[...], h, (((1,), (1,)), ((), ())),
                                  preferred_element_type=F32)


def norm_modulate(x, g, mod, mod_row, shift_idx, w_dt_t=None, ts=1024):
    b, t, d = x.shape
    ts = min(ts, t)
    per_seq = t // ts
    in_specs = [pl.BlockSpec((None, ts, d), lambda i, j: (i, j, 0)),
                pl.BlockSpec((1, d), lambda i, j: (0, 0)),
                pl.BlockSpec((None, 1, d), lambda i, j: (mod_row(i), 0, shift_idx)),
                pl.BlockSpec((None, 1, d), lambda i, j: (mod_row(i), 0, shift_idx + 1))]
    out_specs = pl.BlockSpec((None, ts, d), lambda i, j: (i, j, 0))
    out_shape = jax.ShapeDtypeStruct((b, t, d), BF16)
    args = [x, g, mod, mod]
    if w_dt_t is not None:
        in_specs.append(pl.BlockSpec((LANES, d), lambda i, j: (0, 0)))
        out_specs = [out_specs, pl.BlockSpec((LANES, ts), lambda i, j: (0, i * per_seq + j))]
        out_shape = [out_shape, jax.ShapeDtypeStruct((LANES, b * t), F32)]
        args.append(w_dt_t)
    return pl.pallas_call(
        _norm_mod_kernel,
        grid=(b, t // ts),
        in_specs=in_specs,
        out_specs=out_specs,
        out_shape=out_shape,
        compiler_params=_params("parallel", "parallel"),
        name="norm_mod",
    )(*args)


def _final_norm_kernel(x_ref, g_ref, o_ref):
    x = x_ref[...]
    o_ref[...] = x * lax.rsqrt(jnp.mean(x * x, axis=-1, keepdims=True) + EPS) * g_ref[...]


def final_norm(x, g, ts=1024):
    b, t, d = x.shape
    ts = min(ts, t)
    return pl.pallas_call(
        _final_norm_kernel,
        grid=(b, t // ts),
        in_specs=[pl.BlockSpec((None, ts, d), lambda i, j: (i, j, 0)),
                  pl.BlockSpec((1, d), lambda i, j: (0, 0))],
        out_specs=pl.BlockSpec((None, ts, d), lambda i, j: (i, j, 0)),
        out_shape=jax.ShapeDtypeStruct((b, t, d), F32),
        compiler_params=_params("parallel", "parallel"),
        name="final_norm",
    )(x, g)


CONV_ROWS = 128
CONV_HALO = 16
CONV_UNROLL = 16


def _conv_silu_kernel(u_ref, w_ref, b_ref, o_ref, ext_ref, *, unroll):
    t, tc = u_ref.shape
    pad = CONV_K // 2
    span = CONV_ROWS + 2 * CONV_HALO
    halo = jnp.zeros((CONV_HALO, tc), BF16)
    ext_ref[0:CONV_HALO, :] = halo
    ext_ref[CONV_HALO + t:, :] = halo
    ext_ref[CONV_HALO:CONV_HALO + t, :] = u_ref[...]
    taps = [k - pad for k in range(CONV_K) if k != pad]
    ri = lax.broadcasted_iota(jnp.int32, (len(taps) * CONV_ROWS, span), 0)
    ci = lax.broadcasted_iota(jnp.int32, (len(taps) * CONV_ROWS, span), 1)
    tap_off = jnp.zeros_like(ri)
    for n, off in enumerate(taps):
        tap_off = jnp.where(ri // CONV_ROWS == n, off, tap_off)
    shift_mat = (ci == (ri % CONV_ROWS) + CONV_HALO + tap_off).astype(BF16)
    bias = b_ref[...]
    w = [w_ref[k:k + 1, :] for k in range(CONV_K)]

    def body(it, carry):
        windows, shifted = [], []
        for n in range(unroll):
            r0 = pl.multiple_of((it * unroll + n) * CONV_ROWS, CONV_ROWS)
            win = ext_ref[pl.ds(r0, span), :]
            windows.append((r0, win))
            shifted.append(jnp.dot(shift_mat, win, preferred_element_type=F32))
        for (r0, win), sh in zip(windows, shifted):
            acc = bias + w[pad] * win[CONV_HALO:CONV_HALO + CONV_ROWS].astype(F32)
            for n, off in enumerate(taps):
                acc = acc + w[pad + off] * sh[n * CONV_ROWS:(n + 1) * CONV_ROWS]
            o_ref[pl.ds(r0, CONV_ROWS), :] = (acc * jax.nn.sigmoid(acc)).astype(o_ref.dtype)
        return carry

    lax.fori_loop(0, t // (CONV_ROWS * unroll), body, 0)


def conv_silu(p, conv_w, conv_b):
    b, t, _ = p.shape
    tc = 512 if t >= 1024 else 1024
    unroll = min(CONV_UNROLL, t // CONV_ROWS)
    assert t % (CONV_ROWS * unroll) == 0
    c0 = KVX_XBC // tc
    return pl.pallas_call(
        functools.partial(_conv_silu_kernel, unroll=unroll),
        grid=(b, CONV_CH // tc),
        in_specs=[pl.BlockSpec((None, t, tc), lambda i, j: (i, 0, c0 + j)),
                  pl.BlockSpec((CONV_K, tc), lambda i, j: (0, j)),
                  pl.BlockSpec((1, tc), lambda i, j: (0, j))],
        out_specs=pl.BlockSpec((None, t, tc), lambda i, j: (i, 0, j)),
        out_shape=jax.ShapeDtypeStruct((b, t, CONV_CH), BF16),
        scratch_shapes=[pltpu.VMEM((t + 2 * CONV_HALO, tc), BF16)],
        compiler_params=_params("parallel", "parallel"),
        name="conv_silu",
    )(p, conv_w, conv_b)


def _rope(u, cos, sin_next, sin_prev):
    quarter = HEAD_DIM // 4
    return (u * cos + pltpu.roll(u, HEAD_DIM - quarter, axis=1) * sin_next
            + pltpu.roll(u, quarter, axis=1) * sin_prev)


def _softmax_pv(s, sink_col, v):
    m = jnp.maximum(jnp.max(s, axis=-1, keepdims=True), sink_col)
    e = jnp.exp(s - m)
    denom = jnp.sum(e, axis=-1, keepdims=True) + jnp.exp(sink_col - m)
    o = jnp.dot(e.astype(BF16), v, preferred_element_type=F32)
    return o / denom


def _sink_column(sink_ref, h, rows):
    blk = lax.broadcasted_iota(jnp.int32, (REP * rows, 1), 0) // rows
    col = jnp.zeros((REP * rows, 1), F32)
    for r in range(REP):
        col = jnp.where(blk == r, sink_ref[h * REP + r], col)
    return col


def _latent_attn_kernel(sink_ref, q_ref, kp_ref, kc_ref, kn_ref, vp_ref, vc_ref, vn_ref,
                        kx_ref, vx_ref, qcos_ref, qsn_ref, qsp_ref, cos_ref, sn_ref, sp_ref, o_ref,
                        *, nb):
    n = pl.program_id(1)
    blk = ATT_BLOCK

    def rows(ref, i):
        return ref[pl.ds(pl.multiple_of(i * blk, blk), blk), :]

    n_prev = jnp.maximum(n - 1, 0)
    n_next = jnp.minimum(n + 1, nb - 1)
    k_tables = [[rows(ref, i) for ref in (cos_ref, sn_ref, sp_ref)] for i in (n_prev, n, n_next)]
    qcos, qsn, qsp = rows(qcos_ref, n), rows(qsn_ref, n), rows(qsp_ref, n)
    qi = lax.broadcasted_iota(jnp.int32, (blk, blk), 0)
    kj = lax.broadcasted_iota(jnp.int32, (blk, blk), 1)
    see_prev = kj >= qi + (n == 0).astype(jnp.int32) * blk
    see_next = kj <= qi - (n == nb - 1).astype(jnp.int32) * blk
    def head_cols(h):
        return [slice((h * REP + r) * HEAD_DIM, (h * REP + r + 1) * HEAD_DIM) for r in range(REP)]

    def scores(h):
        kv = slice(h * HEAD_DIM, (h + 1) * HEAD_DIM)
        keys = jnp.concatenate(
            [_rope(ref[:, kv].astype(F32), *tabs).astype(BF16)
             for ref, tabs in zip((kp_ref, kc_ref, kn_ref), k_tables)] + [kx_ref[:, kv]], axis=0)
        return [lax.dot_general(_rope(q_ref[:, cols].astype(F32), qcos, qsn, qsp).astype(BF16),
                                keys, (((1,), (1,)), ((), ())), preferred_element_type=F32)
                for cols in head_cols(h)]

    def softmax_terms(h, logits):
        probs, sink_terms = [], []
        for r, s in enumerate(logits):
            s = jnp.concatenate(
                [jnp.where(see_prev, s[:, :blk], -jnp.inf), s[:, blk:2 * blk],
                 jnp.where(see_next, s[:, 2 * blk:3 * blk], -jnp.inf), s[:, 3 * blk:]], axis=1)
            sink = sink_ref[h * REP + r] * LOG2E
            m = jnp.maximum(jnp.max(s, axis=-1, keepdims=True), sink)
            probs.append(jnp.exp2(s - m).astype(BF16))
            sink_terms.append(jnp.exp2(sink - m))
        return probs, sink_terms

    def outputs(h, probs, sink_terms):
        kv = slice(h * HEAD_DIM, (h + 1) * HEAD_DIM)
        vals = jnp.concatenate([vp_ref[:, kv], vc_ref[:, kv], vn_ref[:, kv], vx_ref[:, kv]], axis=0)
        vals = jnp.concatenate([vals, jnp.ones_like(vals)], axis=1)
        for cols, e, sink_term in zip(head_cols(h), probs, sink_terms):
            o = jnp.dot(e, vals, preferred_element_type=F32)
            o_ref[:, cols] = (o[:, :HEAD_DIM] / (o[:, HEAD_DIM:] + sink_term)).astype(o_ref.dtype)

    logits = scores(0)
    for h in range(N_KV_HEADS):
        ahead = scores(h + 1) if h + 1 < N_KV_HEADS else None
        outputs(h, *softmax_terms(h, logits))
        logits = ahead


def latent_attention(q, kvx, kvx_ctx, sink, q_tables, k_tables):
    b, s, _ = q.shape
    l = kvx_ctx.shape[1]
    nb = s // ATT_BLOCK
    blk = ATT_BLOCK
    kb = kxb = 0
    vb = vxb = KVX_V // KV_WIDTH
    prev = lambda n: jnp.maximum(n - 1, 0)
    nxt = lambda n: jnp.minimum(n + 1, nb - 1)
    kv_spec = lambda cb, f: pl.BlockSpec((None, blk, KV_WIDTH), lambda i, n: (i, f(n), cb))
    table = pl.BlockSpec((s, HEAD_DIM), lambda i, n: (0, 0))
    return pl.pallas_call(
        functools.partial(_latent_attn_kernel, nb=nb),
        grid=(b, nb),
        in_specs=[pl.BlockSpec(memory_space=pltpu.SMEM),
                  pl.BlockSpec((None, blk, ATT_WIDTH), lambda i, n: (i, n, 0)),
                  kv_spec(kb, prev), kv_spec(kb, lambda n: n), kv_spec(kb, nxt),
                  kv_spec(vb, prev), kv_spec(vb, lambda n: n), kv_spec(vb, nxt),
                  pl.BlockSpec((None, l, KV_WIDTH), lambda i, n: (i, 0, kxb)),
                  pl.BlockSpec((None, l, KV_WIDTH), lambda i, n: (i, 0, vxb)),
                  table, table, table, table, table, table],
        out_specs=pl.BlockSpec((None, blk, ATT_WIDTH), lambda i, n: (i, n, 0)),
        out_shape=jax.ShapeDtypeStruct((b, s, ATT_WIDTH), BF16),
        compiler_params=_params("parallel", "parallel"),
        name="latent_attn",
    )(sink, q, kvx, kvx, kvx, kvx, kvx, kvx, kvx_ctx, kvx_ctx, *q_tables, *k_tables)


def _context_attn_kernel(sink_ref, q_ref, k_ref, v_ref, o_ref):
    h = pl.program_id(2)
    blk = q_ref.shape[0]
    q = jnp.concatenate([q_ref[:, r * HEAD_DIM:(r + 1) * HEAD_DIM] for r in range(REP)], axis=0)
    s = lax.dot_general(q, k_ref[...], (((1,), (1,)), ((), ())), preferred_element_type=F32)
    o = _softmax_pv(s * HEAD_DIM ** -0.5, _sink_column(sink_ref, h, blk), v_ref[...])
    for r in range(REP):
        o_ref[:, r * HEAD_DIM:(r + 1) * HEAD_DIM] = o[r * blk:(r + 1) * blk].astype(o_ref.dtype)


def context_attention(q, kvx, sink):
    b, l, _ = q.shape
    blk = min(ATT_BLOCK, l)
    kb, vb, qb = 0, KVX_V // HEAD_DIM, 0
    return pl.pallas_call(
        _context_attn_kernel,
        grid=(b, l // blk, N_KV_HEADS),
        in_specs=[pl.BlockSpec(memory_space=pltpu.SMEM),
                  pl.BlockSpec((None, blk, REP * HEAD_DIM), lambda i, n, h: (i, n, qb + h)),
                  pl.BlockSpec((None, l, HEAD_DIM), lambda i, n, h: (i, 0, kb + h)),
                  pl.BlockSpec((None, l, HEAD_DIM), lambda i, n, h: (i, 0, vb + h))],
        out_specs=pl.BlockSpec((None, blk, REP * HEAD_DIM), lambda i, n, h: (i, n, h)),
        out_shape=jax.ShapeDtypeStruct((b, l, ATT_WIDTH), BF16),
        compiler_params=_params("parallel", "parallel", "arbitrary"),
        name="context_attn",
    )(sink, q, kvx, kvx)


def _split3(x):
    hi = x.astype(BF16).astype(F32)
    r1 = x - hi
    mid = r1.astype(BF16).astype(F32)
    lo = (r1 - mid).astype(BF16).astype(F32)
    return hi, mid, lo


def _softplus(x):
    return jnp.maximum(x, 0.0) + jnp.log1p(jnp.exp(-jnp.abs(x)))


def _ssd_kernel(*refs, nc, zero_init, with_output):
    refs = list(refs)
    x_ref, b_ref, c_ref, dtt_ref, bias_ref, alog_ref = refs[:6]
    del refs[:6]
    z_ref, dskip_ref = (refs.pop(0), refs.pop(0)) if with_output else (None, None)
    init_ref = None if zero_init else refs.pop(0)
    u_ref, ssq_ref = (refs.pop(0), refs.pop(0)) if with_output else (None, None)
    fin_ref, state_ref, rhs_ref, decay_ref, exit_ref, entry_ref = refs[:6]
    y_ref = refs[6] if with_output else None
    q = SSD_CHUNK
    e_heads = HEADS_PER_GROUP
    wide = e_heads * q
    ii = lax.broadcasted_iota(jnp.int32, (q, q), 0)
    jj = lax.broadcasted_iota(jnp.int32, (q, q), 1)
    eye = (ii == jj).astype(BF16)
    left_b = (jj < SSD_HEAD_DIM).astype(BF16)
    right_b = (jj >= SSD_HEAD_DIM).astype(BF16)
    r32 = lax.broadcasted_iota(jnp.int32, (32, wide), 0)
    c32 = lax.broadcasted_iota(jnp.int32, (32, wide), 1) // q
    head_rows = ((r32 < 3 * e_heads) & ((r32 & (e_heads - 1)) == c32)).astype(F32)
    r8 = lax.broadcasted_iota(jnp.int32, (e_heads, wide), 0)
    c8 = lax.broadcasted_iota(jnp.int32, (e_heads, wide), 1) // q
    own_block = (r8 == c8).astype(F32)
    re = lax.broadcasted_iota(jnp.int32, (q, GROUP_WIDTH), 0)
    ce = lax.broadcasted_iota(jnp.int32, (q, GROUP_WIDTH), 1) // SSD_HEAD_DIM
    expand = ((re >= 64) & (re < 64 + 3 * e_heads) & ((re & (e_heads - 1)) == ce)).astype(BF16)
    zeros8 = jnp.zeros((e_heads, q), F32)
    ones24 = jnp.ones((3 * e_heads, q), F32)

    head_params = []
    for d in range(2):
        lo, hi = d * e_heads, (d + 1) * e_heads
        visible = (jj >= ii) if d else (jj <= ii)
        cum_mat = ((jj <= ii) if d else (jj >= ii)).astype(BF16)
        head_params.append((bias_ref[lo:hi, :], -jnp.exp(alog_ref[lo:hi, :]), cum_mat))

        @pl.when(pl.program_id(1) == 0)
        def _(d=d, visible=visible):
            rhs_ref[d, 0:q, :] = jnp.tile(jnp.where(visible, 0.0, NEG_BIG),
                                          (1, e_heads)).astype(BF16)
            rhs_ref[d, q:q + 32, :] = head_rows.astype(BF16)
            rhs_ref[d, q + 32:2 * q, :] = jnp.zeros((q - 32, wide), BF16)

        if zero_init:
            state_ref[d] = jnp.zeros(state_ref.shape[1:], F32)
        else:
            state_ref[d] = init_ref[d]

    def chunk_rows(d, step):
        c = (nc - 1 - step) if d else step
        return pl.ds(c * q if isinstance(c, int) else pl.multiple_of(c * q, q), q)

    def prepare_cumsum(d, step):
        lo, hi = d * e_heads, (d + 1) * e_heads
        bias_t, aneg_t, cum_mat = head_params[d]
        rows = chunk_rows(d, step)
        dt = _softplus(dtt_ref[lo:hi, rows] + bias_t)
        da = jnp.concatenate(_split3(dt * aneg_t), axis=0).astype(BF16)
        return dict(d=d, rows=rows, dt=dt, cs=jnp.dot(da, cum_mat, preferred_element_type=F32))

    def prepare_matmuls(v):
        d, dt, cs = v["d"], v["dt"], v["cs"]
        last = 0 if d else q - 1
        bc = b_ref[v["rows"], :]
        cc = c_ref[v["rows"], :]
        a2 = (cs[0:e_heads] + cs[e_heads:2 * e_heads] + cs[2 * e_heads:]) * LOG2E
        ea = jnp.exp2(a2)
        w_exit = jnp.exp2(a2[:, last:last + 1] - a2) * dt
        g = jnp.maximum(jnp.log2(dt), LOG2_DT_FLOOR) - a2
        rhs_ref[d, q + 32:q + 64, :] = jnp.concatenate(
            [jnp.tile(part, (1, e_heads)) * own_block for part in _split3(g)]
            + [jnp.zeros((e_heads, wide), F32)], axis=0).astype(BF16)
        split_t = jnp.concatenate(list(_split3(a2)) + [zeros8, ones24, zeros8]
                                  + list(_split3(ea)) + [jnp.zeros((q - 88, q), F32)], axis=0)
        split_n = split_t.T.astype(BF16)
        v["seg"] = jnp.dot(jnp.concatenate([eye, split_n], axis=1), rhs_ref[d],
                           preferred_element_type=F32)
        v["ea_wide"] = jnp.dot(split_n, expand, preferred_element_type=F32)
        v["cb"] = lax.dot_general(cc, bc, (((1,), (1,)), ((), ())), preferred_element_type=F32)
        b_t = bc.astype(F32).T
        v["b_exit"] = jnp.concatenate([b_t * w_exit[e:e + 1, :] for e in range(e_heads)],
                                      axis=1).astype(BF16)

    def prepare_store(v, slot):
        d = v["d"]
        decay_ref[slot, d] = (jnp.tile(v["cb"], (1, e_heads)) * jnp.exp2(v["seg"])).astype(BF16)
        exit_ref[slot, d] = v["b_exit"]
        entry_ref[slot, d] = v["ea_wide"]

    def emit_entry(d, step):
        rows = chunk_rows(d, step)
        state = state_ref[d]
        w = dict(d=d, rows=rows, state=state)
        if with_output:
            w["y_off"] = jnp.dot(c_ref[rows, :], state.astype(BF16), preferred_element_type=F32)
        return w

    def emit_outputs(w, slot, second_visit):
        d, rows = w["d"], w["rows"]
        last = 0 if d else q - 1
        xc = x_ref[rows, :]
        ssq = jnp.zeros((q, LANES), F32)
        for p in range(e_heads // 2):
            cols = slice(p * LANES, (p + 1) * LANES)
            two = slice(2 * p * q, 2 * (p + 1) * q)
            xp = xc[:, cols]
            x_pair = jnp.concatenate([xp * left_b, xp * right_b], axis=0)
            entry = entry_ref[slot, d, :, cols]
            if not with_output:
                new = jnp.dot(exit_ref[slot, d, :, two], x_pair, preferred_element_type=F32)
                state_ref[d, :, cols] = w["state"][:, cols] * entry[last:last + 1, :] + new
                continue
            lhs = jnp.concatenate([decay_ref[slot, d, :, two], exit_ref[slot, d, :, two]], axis=0)
            out = jnp.dot(lhs, x_pair, preferred_element_type=F32)
            state_ref[d, :, cols] = w["state"][:, cols] * entry[last:last + 1, :] + out[q:]
            y = out[:q] + w["y_off"][:, cols] * entry
            if not second_visit:
                y_ref[rows, cols] = y
                continue
            z = z_ref[rows, cols].astype(F32)
            y = y_ref[rows, cols] + y + xp.astype(F32) * dskip_ref[:, cols]
            gated = y * (z * jax.nn.sigmoid(z))
            u_ref[rows, cols] = gated.astype(u_ref.dtype)
            ssq = ssq + gated * gated
        if with_output and second_visit:
            ssq_ref[rows, :] = ssq

    def prepare_all(step, slot):
        chains = [prepare_cumsum(d, step) for d in range(2)]
        for v in chains:
            prepare_matmuls(v)
        for v in chains:
            prepare_store(v, slot)

    def body(second_visits, final=False):
        def run(it, carry):
            for half in range(2):
                step = 2 * it + half
                look_ahead = not (final and half)
                chains = [prepare_cumsum(d, step + 1) for d in range(2)] if look_ahead else []
                entries = [emit_entry(d, step) for d in range(2)]
                for n, w in enumerate(entries):
                    emit_outputs(w, half, second_visits[half])
                    if look_ahead:
                        prepare_matmuls(chains[n])
                for v in chains:
                    prepare_store(v, 1 - half)
            return carry
        return run

    prepare_all(0, 0)
    n_it = nc // 2
    if nc == 2:
        body((False, True), final=True)(0, 0)
    else:
        lax.fori_loop(0, n_it // 2, body((False, False)), 0)
        lax.fori_loop(n_it // 2, n_it - 1, body((True, True)), 0)
        body((True, True), final=True)(n_it - 1, 0)
    for d in range(2):
        fin_ref[d] = state_ref[d]


def ssd_scan(xbc, dt_t, bias_t, alog_t, init, gate=None):
    b, t, _ = xbc.shape
    nc = t // SSD_CHUNK
    assert nc == 2 or nc % 4 == 0
    wide = HEADS_PER_GROUP * SSD_CHUNK
    g = SSD_GROUPS
    e2 = 2 * HEADS_PER_GROUP
    bm0 = D_INNER // D_STATE
    cm0 = bm0 + g
    state_shape = (2, D_STATE, GROUP_WIDTH)
    group_block = pl.BlockSpec((None, t, GROUP_WIDTH), lambda i, j: (i, 0, j))
    state_block = pl.BlockSpec((None, None) + state_shape, lambda i, j: (i, j, 0, 0, 0))
    in_specs = [group_block,
                pl.BlockSpec((None, t, D_STATE), lambda i, j: (i, 0, bm0 + j)),
                pl.BlockSpec((None, t, D_STATE), lambda i, j: (i, 0, cm0 + j)),
                pl.BlockSpec((e2, t), lambda i, j: (j, i)),
                pl.BlockSpec((None, e2, 1), lambda i, j: (j, 0, 0)),
                pl.BlockSpec((None, e2, 1), lambda i, j: (j, 0, 0))]
    args = [xbc, xbc, xbc, dt_t, bias_t, alog_t]
    out_specs, out_shape = [state_block], [jax.ShapeDtypeStruct((b, g) + state_shape, F32)]
    scratch = [pltpu.VMEM(state_shape, F32),
               pltpu.VMEM((2, 2 * SSD_CHUNK, wide), BF16),
               pltpu.VMEM((2, 2, SSD_CHUNK, wide), BF16),
               pltpu.VMEM((2, 2, D_STATE, wide), BF16),
               pltpu.VMEM((2, 2, SSD_CHUNK, GROUP_WIDTH), F32)]
    if gate is not None:
        p, dskip = gate
        in_specs += [group_block,
                     pl.BlockSpec((None, 1, GROUP_WIDTH), lambda i, j: (j, 0, 0))]
        args += [p, dskip]
        out_specs = [group_block, pl.BlockSpec((None, t, LANES), lambda i, j: (i, 0, j))] + out_specs
        out_shape = [jax.ShapeDtypeStruct((b, t, D_INNER), BF16),
                     jax.ShapeDtypeStruct((b, t, g * LANES), F32)] + out_shape
        scratch.append(pltpu.VMEM((t, GROUP_WIDTH), F32))
    if init is not None:
        in_specs.append(state_block)
        args.append(init)
    outs = pl.pallas_call(
        functools.partial(_ssd_kernel, nc=nc, zero_init=init is None, with_output=gate is not None),
        grid=(b, g),
        in_specs=in_specs,
        out_specs=out_specs,
        out_shape=out_shape,
        scratch_shapes=scratch,
        compiler_params=_params("parallel", "arbitrary"),
        name="ssd_scan",
    )(*args)
    return tuple(outs) if gate is not None else (None, None, outs[0])


def _rope_tables(seq):
    t = jnp.arange(seq)
    row = (t // GRID_W).astype(F32)
    col = (t % GRID_W).astype(F32)
    axis_dim = HEAD_DIM // 2
    inv_freq = ROPE_THETA ** (-jnp.arange(0, axis_dim, 2, dtype=F32) / axis_dim)
    ang_r = row[:, None] * inv_freq[None]
    ang_c = col[:, None] * inv_freq[None]
    cos_r, sin_r, cos_c, sin_c = jnp.cos(ang_r), jnp.sin(ang_r), jnp.cos(ang_c), jnp.sin(ang_c)
    zero = jnp.zeros_like(sin_r)
    cos = jnp.concatenate([cos_r, cos_r, cos_c, cos_c], axis=-1)
    sin_next = jnp.concatenate([-sin_r, zero, -sin_c, zero], axis=-1)
    sin_prev = jnp.concatenate([zero, sin_r, zero, sin_c], axis=-1)
    return cos, sin_next, sin_prev


def _group_major(v):
    return v.reshape(2, SSD_GROUPS, HEADS_PER_GROUP).transpose(1, 0, 2).reshape(
        SSD_GROUPS, 2 * HEADS_PER_GROUP)


def _project(h2, w, shape, name):
    n_out = _weight_array(w).shape[-1]
    return matmul([(h2, w)], [], lambda parts, ex: parts[0], n_out, BF16, tm=1024, tn=1024,
                  name=name).reshape(shape + (n_out,))


def _mixer_side(h, dt_t, w_kvx, w_q, w_zg, conv_w, conv_b, ssd_params, init, dskip):
    b, t, d = h.shape
    h2 = h.reshape(b * t, d)
    kvx = _project(h2, w_kvx, (b, t), "in_proj_kvx")
    q = zg = gate = None
    if dskip is not None:
        q = _project(h2, w_q, (b, t), "in_proj_q")
        zg = _project(h2, w_zg, (b, t), "in_proj_zg")
        gate = (zg, dskip)
    xbc = conv_silu(kvx, conv_w, conv_b)
    u, ssq, fin = ssd_scan(xbc, dt_t, *ssd_params, init, gate)
    return (kvx, q, zg), u, ssq, fin


def _merge_out(att, u, ssq, zg, w_o_attn, w_o_ssd_g, w_out, x, mod, mod_row):
    b, t, d = x.shape
    m = b * t
    tm, tn = min(512, t), 1024
    p2 = zg.reshape(m, zg.shape[-1])
    ga0, gs0 = ZG_GATE_A // tn, ZG_GATE_S // tn
    n_part = ssq.shape[-1]

    def merge(parts, ex):
        ga, gs, sq = ex
        inv_rms = lax.rsqrt(jnp.sum(sq, axis=-1, keepdims=True) * (1.0 / D_INNER) + EPS)
        return (jax.nn.sigmoid(ga.astype(F32)) * parts[0]
                + jax.nn.sigmoid(gs.astype(F32)) * (inv_rms * parts[1]))

    merged = matmul([(att.reshape(m, ATT_WIDTH), w_o_attn), (u.reshape(m, D_INNER), w_o_ssd_g)],
                    [(p2, (tm, tn), lambda i, j: (i, ga0 + j)),
                     (p2, (tm, tn), lambda i, j: (i, gs0 + j)),
                     (ssq.reshape(m, n_part), (tm, n_part), lambda i, j: (i, 0))],
                    merge, d, BF16, tm=tm, tn=tn, w_stationary=True, name="merge")
    return _residual_matmul(merged, w_out, x, mod, mod_row, 2, name="out_proj")


def _residual_matmul(a, w, x, mod, mod_row, gate_idx, name):
    b, t, d = x.shape
    ktiled = a.shape[1] > 4096
    tm, tn = (min(1024, t), 1024) if ktiled else (min(512, t), d)
    per_seq = t // tm
    nj = d // tn
    extras = [(x.reshape(b * t, d), (tm, tn), lambda i, j: (i, j)),
              (mod, (None, 1, tn), lambda i, j: (mod_row(i // per_seq), 0, gate_idx * nj + j))]
    res = lambda parts, ex: ex[0] + ex[1] * parts[0]
    if ktiled:
        out = matmul_ktiled(a, w, extras, res, F32, tm=tm, tn=tn, tk=2048, name=name)
    else:
        out = matmul([(a, w)], extras, res, d, F32, tm=tm, tn=tn, name=name)
    return out.reshape(b, t, d)


def _mlp(x, g, w1, w2, mod, mod_row):
    b, t, d = x.shape
    h = norm_modulate(x, g, mod, mod_row, 3).reshape(b * t, d)
    hid = matmul([(h, w1)], [], lambda parts, ex: jnp.square(jnp.maximum(parts[0], 0.0)), D_FF,
                 BF16, tm=1024, tn=1024, name="ff1")
    return _residual_matmul(hid, w2, x, mod, mod_row, 5, name="ff2")


def kernel(x, c, ctx, c_ctx, w_ada, b_ada, g_norm1, g_norm2, w_in, attn_sink, conv_w, conv_b,
           dt_bias, a_log, d_skip, g_ssd, w_o_attn, w_o_ssd, w_out, w_ff1, w_ff2, g_final):
    depth = w_in.shape[0]
    batch, seq, d = x.shape
    q_scale = LOG2E * HEAD_DIM ** -0.5
    k_tables = _rope_tables(seq)
    q_tables = tuple(tb * q_scale for tb in k_tables)

    n_rows = -(-(batch + 1) // 8) * 8
    cond = jnp.zeros((n_rows, d), F32).at[:batch].set(c).at[batch].set(c_ctx)
    mod = adaln_modulation(cond, w_ada, b_ada[:, None, :]).reshape(depth * n_rows, 1, N_MOD * d)
    w_kvx_all = w_in[:, :, :REF_COL_DT].astype(BF16)
    w_q_all = w_in[:, :, REF_COL_Q:REF_COL_Z].astype(BF16)
    w_zg_all = w_in[:, :, REF_COL_Z:].astype(BF16)
    wa_all, wo_all = w_o_attn.astype(BF16), w_out.astype(BF16)
    ws_all = (g_ssd[:, :, None] * w_o_ssd).astype(BF16)
    w1_all, w2_all = w_ff1.astype(BF16), w_ff2.astype(BF16)

    h_ctx = ctx
    for i in range(depth):
        ctx_out = i < depth - 1
        latent_row = lambda b, i=i: i * n_rows + b
        ctx_row = lambda b, i=i: i * n_rows + batch
        w_kvx, w_q, w_zg = (w_kvx_all, i), (w_q_all, i), (w_zg_all, i)
        wa, ws, wo, w1, w2 = (wa_all, i), (ws_all, i), (wo_all, i), (w1_all, i), (w2_all, i)
        w_dt_t = w_in[i][:, REF_COL_DT:REF_COL_Q].reshape(
            d, 2, SSD_GROUPS, HEADS_PER_GROUP).transpose(2, 1, 3, 0).reshape(
                2 * SSD_HEADS, d).astype(BF16)
        ssd_params = (_group_major(dt_bias[i])[:, :, None], _group_major(a_log[i])[:, :, None])
        dskip = jnp.repeat(d_skip[i][0] + d_skip[i][1], SSD_HEAD_DIM).reshape(
            SSD_GROUPS, 1, GROUP_WIDTH)
        sink = attn_sink[i]
        g1, g2 = g_norm1[i][None], g_norm2[i][None]

        hc, dt_c = norm_modulate(h_ctx, g1, mod, ctx_row, 0, w_dt_t)
        (kvx_c, q_c, zg_c), u_c, ssq_c, fin_c = _mixer_side(
            hc, dt_c, w_kvx, w_q if ctx_out else None, w_zg if ctx_out else None, conv_w[i],
            conv_b[i][None], ssd_params, None, dskip if ctx_out else None)
        hx, dt_x = norm_modulate(x, g1, mod, latent_row, 0, w_dt_t)
        (kvx_x, q_x, zg_x), u_x, ssq_x, _ = _mixer_side(
            hx, dt_x, w_kvx, w_q, w_zg, conv_w[i], conv_b[i][None], ssd_params, fin_c, dskip)
        att_x = latent_attention(q_x, kvx_x, kvx_c, sink, q_tables, k_tables)
        x = _merge_out(att_x, u_x, ssq_x, zg_x, wa, ws, wo, x, mod, latent_row)
        x = _mlp(x, g2, w1, w2, mod, latent_row)
        if ctx_out:
            att_c = context_attention(q_c, kvx_c, sink)
            h_ctx = _merge_out(att_c, u_c, ssq_c, zg_c, wa, ws, wo, h_ctx, mod, ctx_row)
            h_ctx = _mlp(h_ctx, g2, w1, w2, mod, ctx_row)
    return final_norm(x, g_final[None])
```

```python
import functools

import jax
import jax.numpy as jnp
from jax import lax
from jax.experimental import pallas as pl
from jax.experimental.pallas import tpu as pltpu

F32 = jnp.float32
BF16 = jnp.bfloat16

D_MODEL = 2048
GRID_W = 64
N_HEADS = 16
N_KV_HEADS = 4
HEAD_DIM = 128
REP = N_HEADS // N_KV_HEADS
ATT_WIDTH = N_HEADS * HEAD_DIM
KV_WIDTH = N_KV_HEADS * HEAD_DIM
WINDOW = 128
ATT_BLOCK = 128
ROPE_THETA = 10000.0
D_INNER = 2 * D_MODEL
SSD_HEAD_DIM = 64
SSD_HEADS = D_INNER // SSD_HEAD_DIM
SSD_GROUPS = 8
HEADS_PER_GROUP = SSD_HEADS // SSD_GROUPS
D_STATE = 128
CONV_K = 5
SSD_CHUNK = 128
GROUP_WIDTH = HEADS_PER_GROUP * SSD_HEAD_DIM
CONV_CH = D_INNER + 2 * SSD_GROUPS * D_STATE
D_FF = 4 * D_MODEL
N_MOD = 6
EPS = 1e-6
LOG2E = 1.4426950408889634
NEG_BIG = -1e30
LOG2_DT_FLOOR = -1e4

REF_COL_XBC = 2 * KV_WIDTH
REF_COL_DT = REF_COL_XBC + CONV_CH
REF_COL_Q = REF_COL_DT + 2 * SSD_HEADS
REF_COL_Z = REF_COL_Q + ATT_WIDTH
REF_COL_GATE = REF_COL_Z + D_INNER

KVX_V = KV_WIDTH
KVX_XBC = 2 * KV_WIDTH
ZG_GATE_A = D_INNER
ZG_GATE_S = D_INNER + D_MODEL

LANES = 128
VMEM_LIMIT = 56 * 1024 * 1024


def _params(*sem):
    return pltpu.CompilerParams(dimension_semantics=sem, vmem_limit_bytes=VMEM_LIMIT)


def _mm_kernel(*refs, n_pairs, n_extra, epilogue):
    a_refs = refs[:n_pairs]
    w_refs = refs[n_pairs:2 * n_pairs]
    extra = refs[2 * n_pairs:2 * n_pairs + n_extra]
    o_ref = refs[2 * n_pairs + n_extra]
    parts = [jnp.dot(a[...], w[...], preferred_element_type=F32) for a, w in zip(a_refs, w_refs)]
    o_ref[...] = epilogue(parts, [e[...] for e in extra]).astype(o_ref.dtype)


def _weight_array(w):
    return w[0] if isinstance(w, tuple) else w


def _weight_spec(w, tn, index_map, at=lambda im: im, tk=None):
    if not isinstance(w, tuple):
        return pl.BlockSpec((tk or w.shape[0], tn), at(index_map))
    stack, layer = w
    return pl.BlockSpec((None, tk or stack.shape[1], tn),
                        at(lambda *ij: (layer,) + tuple(index_map(*ij))))


def matmul(pairs, extras, epilogue, n_out, out_dtype, tm, tn, w_stationary=False, name="mm"):
    m = pairs[0][0].shape[0]
    tm = min(tm, m)
    assert m % tm == 0 and n_out % tn == 0
    if w_stationary:
        grid = (n_out // tn, m // tm)
        at = lambda im: (lambda j, i: im(i, j))
    else:
        grid = (m // tm, n_out // tn)
        at = lambda im: im
    in_specs = [pl.BlockSpec((tm, a.shape[1]), at(lambda i, j: (i, 0))) for a, _ in pairs]
    in_specs += [_weight_spec(w, tn, lambda i, j: (0, j), at) for _, w in pairs]
    in_specs += [pl.BlockSpec(bs, at(im)) for _, bs, im in extras]
    pairs = [(a, _weight_array(w)) for a, w in pairs]
    kern = functools.partial(_mm_kernel, n_pairs=len(pairs), n_extra=len(extras), epilogue=epilogue)
    return pl.pallas_call(
        kern,
        grid=grid,
        in_specs=in_specs,
        out_specs=pl.BlockSpec((tm, tn), at(lambda i, j: (i, j))),
        out_shape=jax.ShapeDtypeStruct((m, n_out), out_dtype),
        compiler_params=_params("parallel", "arbitrary"),
        name=name,
    )(*[a for a, _ in pairs], *[w for _, w in pairs], *[e for e, _, _ in extras])


def _mm_acc_kernel(*refs, nk, n_extra, epilogue):
    a_ref, w_ref = refs[0], refs[1]
    extra = refs[2:2 + n_extra]
    o_ref, acc_ref = refs[2 + n_extra], refs[3 + n_extra]
    k = pl.program_id(2)

    @pl.when(k == 0)
    def _():
        acc_ref[...] = jnp.zeros_like(acc_ref)

    acc_ref[...] += jnp.dot(a_ref[...], w_ref[...], preferred_element_type=F32)

    @pl.when(k == nk - 1)
    def _():
        o_ref[...] = epilogue([acc_ref[...]], [e[...] for e in extra]).astype(o_ref.dtype)


def matmul_ktiled(a, w, extras, epilogue, out_dtype, tm, tn, tk, name="mmk"):
    m, kdim = a.shape
    n_out = _weight_array(w).shape[-1]
    tm = min(tm, m)
    assert m % tm == 0 and n_out % tn == 0 and kdim % tk == 0
    nk = kdim // tk
    in_specs = [pl.BlockSpec((tm, tk), lambda i, j, k: (i, k)),
                _weight_spec(w, tn, lambda i, j, k: (k, j), tk=tk)]
    in_specs += [pl.BlockSpec(bs, (lambda i, j, k, im=im: im(i, j))) for _, bs, im in extras]
    kern = functools.partial(_mm_acc_kernel, nk=nk, n_extra=len(extras), epilogue=epilogue)
    return pl.pallas_call(
        kern,
        grid=(m // tm, n_out // tn, nk),
        in_specs=in_specs,
        out_specs=pl.BlockSpec((tm, tn), lambda i, j, k: (i, j)),
        out_shape=jax.ShapeDtypeStruct((m, n_out), out_dtype),
        scratch_shapes=[pltpu.VMEM((tm, tn), F32)],
        compiler_params=_params("parallel", "arbitrary", "arbitrary"),
        name=name,
    )(a, _weight_array(w), *[e for e, _, _ in extras])


def _adaln_kernel(c_ref, w_ref, b_ref, o_ref):
    c = c_ref[...]
    silu_c = c * jax.nn.sigmoid(c)
    o_ref[...] = jnp.dot(silu_c, w_ref[...], preferred_element_type=F32,
                         precision=lax.Precision.HIGHEST) + b_ref[...]


def adaln_modulation(cond, w, b, tn=512):
    r, d = cond.shape
    depth, _, n = w.shape
    return pl.pallas_call(
        _adaln_kernel,
        grid=(depth, n // tn),
        in_specs=[pl.BlockSpec((r, d), lambda l, j: (0, 0)),
                  pl.BlockSpec((None, d, tn), lambda l, j: (l, 0, j)),
                  pl.BlockSpec((None, 1, tn), lambda l, j: (l, 0, j))],
        out_specs=pl.BlockSpec((None, r, tn), lambda l, j: (l, 0, j)),
        out_shape=jax.ShapeDtypeStruct((depth, r, n), F32),
        compiler_params=_params("parallel", "arbitrary"),
        name="adaln",
    )(cond, w, b)


def _norm_mod_kernel(x_ref, g_ref, shift_ref, scale_ref, *rest):
    x = x_ref[...]
    y = x * lax.rsqrt(jnp.mean(x * x, axis=-1, keepdims=True) + EPS) * g_ref[...]
    h = (y * (1.0 + scale_ref[...]) + shift_ref[...]).astype(BF16)
    if len(rest) == 1:
        rest[0][...] = h
        return
    wt_ref, o_ref, dt_ref = rest
    o_ref[...] = h
    dt_ref[...] = lax.dot_general(wt_ref[...], h, (((1,), (1,)), ((), ())),
                                  preferred_element_type=F32)


def norm_modulate(x, g, mod, mod_row, shift_idx, w_dt_t=None, ts=1024):
    b, t, d = x.shape
    ts = min(ts, t)
    per_seq = t // ts
    in_specs = [pl.BlockSpec((None, ts, d), lambda i, j: (i, j, 0)),
                pl.BlockSpec((1, d), lambda i, j: (0, 0)),
                pl.BlockSpec((None, 1, d), lambda i, j: (mod_row(i), 0, shift_idx)),
                pl.BlockSpec((None, 1, d), lambda i, j: (mod_row(i), 0, shift_idx + 1))]
    out_specs = pl.BlockSpec((None, ts, d), lambda i, j: (i, j, 0))
    out_shape = jax.ShapeDtypeStruct((b, t, d), BF16)
    args = [x, g, mod, mod]
    if w_dt_t is not None:
        in_specs.append(pl.BlockSpec((LANES, d), lambda i, j: (0, 0)))
        out_specs = [out_specs, pl.BlockSpec((LANES, ts), lambda i, j: (0, i * per_seq + j))]
        out_shape = [out_shape, jax.ShapeDtypeStruct((LANES, b * t), F32)]
        args.append(w_dt_t)
    return pl.pallas_call(
        _norm_mod_kernel,
        grid=(b, t // ts),
        in_specs=in_specs,
        out_specs=out_specs,
        out_shape=out_shape,
        compiler_params=_params("parallel", "parallel"),
        name="norm_mod",
    )(*args)


def _final_norm_kernel(x_ref, g_ref, o_ref):
    x = x_ref[...]
    o_ref[...] = x * lax.rsqrt(jnp.mean(x * x, axis=-1, keepdims=True) + EPS) * g_ref[...]


def final_norm(x, g, ts=1024):
    b, t, d = x.shape
    ts = min(ts, t)
    return pl.pallas_call(
        _final_norm_kernel,
        grid=(b, t // ts),
        in_specs=[pl.BlockSpec((None, ts, d), lambda i, j: (i, j, 0)),
                  pl.BlockSpec((1, d), lambda i, j: (0, 0))],
        out_specs=pl.BlockSpec((None, ts, d), lambda i, j: (i, j, 0)),
        out_shape=jax.ShapeDtypeStruct((b, t, d), F32),
        compiler_params=_params("parallel", "parallel"),
        name="final_norm",
    )(x, g)


CONV_ROWS = 128
CONV_HALO = 16
CONV_UNROLL = 16


def _conv_silu_kernel(u_ref, w_ref, b_ref, o_ref, ext_ref, *, unroll):
    t, tc = u_ref.shape
    pad = CONV_K // 2
    span = CONV_ROWS + 2 * CONV_HALO
    halo = jnp.zeros((CONV_HALO, tc), BF16)
    ext_ref[0:CONV_HALO, :] = halo
    ext_ref[CONV_HALO + t:, :] = halo
    ext_ref[CONV_HALO:CONV_HALO + t, :] = u_ref[...]
    taps = [k - pad for k in range(CONV_K) if k != pad]
    ri = lax.broadcasted_iota(jnp.int32, (len(taps) * CONV_ROWS, span), 0)
    ci = lax.broadcasted_iota(jnp.int32, (len(taps) * CONV_ROWS, span), 1)
    tap_off = jnp.zeros_like(ri)
    for n, off in enumerate(taps):
        tap_off = jnp.where(ri // CONV_ROWS == n, off, tap_off)
    shift_mat = (ci == (ri % CONV_ROWS) + CONV_HALO + tap_off).astype(BF16)
    bias = b_ref[...]
    w = [w_ref[k:k + 1, :] for k in range(CONV_K)]

    def body(it, carry):
        windows, shifted = [], []
        for n in range(unroll):
            r0 = pl.multiple_of((it * unroll + n) * CONV_ROWS, CONV_ROWS)
            win = ext_ref[pl.ds(r0, span), :]
            windows.append((r0, win))
            shifted.append(jnp.dot(shift_mat, win, preferred_element_type=F32))
        for (r0, win), sh in zip(windows, shifted):
            acc = bias + w[pad] * win[CONV_HALO:CONV_HALO + CONV_ROWS].astype(F32)
            for n, off in enumerate(taps):
                acc = acc + w[pad + off] * sh[n * CONV_ROWS:(n + 1) * CONV_ROWS]
            o_ref[pl.ds(r0, CONV_ROWS), :] = (acc * jax.nn.sigmoid(acc)).astype(o_ref.dtype)
        return carry

    lax.fori_loop(0, t // (CONV_ROWS * unroll), body, 0)


def conv_silu(p, conv_w, conv_b):
    b, t, _ = p.shape
    tc = 1024
    unroll = min(CONV_UNROLL, t // CONV_ROWS)
    assert t % (CONV_ROWS * unroll) == 0
    c0 = KVX_XBC // tc
    return pl.pallas_call(
        functools.partial(_conv_silu_kernel, unroll=unroll),
        grid=(b, CONV_CH // tc),
        in_specs=[pl.BlockSpec((None, t, tc), lambda i, j: (i, 0, c0 + j)),
                  pl.BlockSpec((CONV_K, tc), lambda i, j: (0, j)),
                  pl.BlockSpec((1, tc), lambda i, j: (0, j))],
        out_specs=pl.BlockSpec((None, t, tc), lambda i, j: (i, 0, j)),
        out_shape=jax.ShapeDtypeStruct((b, t, CONV_CH), BF16),
        scratch_shapes=[pltpu.VMEM((t + 2 * CONV_HALO, tc), BF16)],
        compiler_params=_params("parallel", "parallel"),
        name="conv_silu",
    )(p, conv_w, conv_b)


def _rope(u, cos, sin_next, sin_prev):
    quarter = HEAD_DIM // 4
    return (u * cos + pltpu.roll(u, HEAD_DIM - quarter, axis=1) * sin_next
            + pltpu.roll(u, quarter, axis=1) * sin_prev)


def _softmax_pv(s, sink_col, v):
    m = jnp.maximum(jnp.max(s, axis=-1, keepdims=True), sink_col)
    e = jnp.exp(s - m)
    denom = jnp.sum(e, axis=-1, keepdims=True) + jnp.exp(sink_col - m)
    o = jnp.dot(e.astype(BF16), v, preferred_element_type=F32)
    return o / denom


def _sink_column(sink_ref, h, rows):
    blk = lax.broadcasted_iota(jnp.int32, (REP * rows, 1), 0) // rows
    col = jnp.zeros((REP * rows, 1), F32)
    for r in range(REP):
        col = jnp.where(blk == r, sink_ref[h * REP + r], col)
    return col


def _latent_attn_kernel(sink_ref, q_ref, kp_ref, kc_ref, kn_ref, vp_ref, vc_ref, vn_ref,
                        kx_ref, vx_ref, qcos_ref, qsn_ref, qsp_ref, cos_ref, sn_ref, sp_ref, o_ref,
                        *, nb):
    n = pl.program_id(1)
    blk = ATT_BLOCK

    def rows(ref, i):
        return ref[pl.ds(pl.multiple_of(i * blk, blk), blk), :]

    n_prev = jnp.maximum(n - 1, 0)
    n_next = jnp.minimum(n + 1, nb - 1)
    k_tables = [[rows(ref, i) for ref in (cos_ref, sn_ref, sp_ref)] for i in (n_prev, n, n_next)]
    qcos, qsn, qsp = rows(qcos_ref, n), rows(qsn_ref, n), rows(qsp_ref, n)
    qi = lax.broadcasted_iota(jnp.int32, (blk, blk), 0)
    kj = lax.broadcasted_iota(jnp.int32, (blk, blk), 1)
    see_prev = kj >= qi + (n == 0).astype(jnp.int32) * blk
    see_next = kj <= qi - (n == nb - 1).astype(jnp.int32) * blk
    def head_cols(h):
        return [slice((h * REP + r) * HEAD_DIM, (h * REP + r + 1) * HEAD_DIM) for r in range(REP)]

    def scores(h):
        kv = slice(h * HEAD_DIM, (h + 1) * HEAD_DIM)
        keys = jnp.concatenate(
            [_rope(ref[:, kv].astype(F32), *tabs).astype(BF16)
             for ref, tabs in zip((kp_ref, kc_ref, kn_ref), k_tables)] + [kx_ref[:, kv]], axis=0)
        return [lax.dot_general(_rope(q_ref[:, cols].astype(F32), qcos, qsn, qsp).astype(BF16),
                                keys, (((1,), (1,)), ((), ())), preferred_element_type=F32)
                for cols in head_cols(h)]

    def softmax_terms(h, logits):
        probs, sink_terms = [], []
        for r, s in enumerate(logits):
            s = jnp.concatenate(
                [jnp.where(see_prev, s[:, :blk], -jnp.inf), s[:, blk:2 * blk],
                 jnp.where(see_next, s[:, 2 * blk:3 * blk], -jnp.inf), s[:, 3 * blk:]], axis=1)
            sink = sink_ref[h * REP + r] * LOG2E
            m = jnp.maximum(jnp.max(s, axis=-1, keepdims=True), sink)
            probs.append(jnp.exp2(s - m).astype(BF16))
            sink_terms.append(jnp.exp2(sink - m))
        return probs, sink_terms

    def outputs(h, probs, sink_terms):
        kv = slice(h * HEAD_DIM, (h + 1) * HEAD_DIM)
        vals = jnp.concatenate([vp_ref[:, kv], vc_ref[:, kv], vn_ref[:, kv], vx_ref[:, kv]], axis=0)
        vals = jnp.concatenate([vals, jnp.ones_like(vals)], axis=1)
        for cols, e, sink_term in zip(head_cols(h), probs, sink_terms):
            o = jnp.dot(e, vals, preferred_element_type=F32)
            o_ref[:, cols] = (o[:, :HEAD_DIM] / (o[:, HEAD_DIM:] + sink_term)).astype(o_ref.dtype)

    logits = scores(0)
    for h in range(N_KV_HEADS):
        ahead = scores(h + 1) if h + 1 < N_KV_HEADS else None
        outputs(h, *softmax_terms(h, logits))
        logits = ahead


def latent_attention(q, kvx, kvx_ctx, sink, q_tables, k_tables):
    b, s, _ = q.shape
    l = kvx_ctx.shape[1]
    nb = s // ATT_BLOCK
    blk = ATT_BLOCK
    kb = kxb = 0
    vb = vxb = KVX_V // KV_WIDTH
    prev = lambda n: jnp.maximum(n - 1, 0)
    nxt = lambda n: jnp.minimum(n + 1, nb - 1)
    kv_spec = lambda cb, f: pl.BlockSpec((None, blk, KV_WIDTH), lambda i, n: (i, f(n), cb))
    table = pl.BlockSpec((s, HEAD_DIM), lambda i, n: (0, 0))
    return pl.pallas_call(
        functools.partial(_latent_attn_kernel, nb=nb),
        grid=(b, nb),
        in_specs=[pl.BlockSpec(memory_space=pltpu.SMEM),
                  pl.BlockSpec((None, blk, ATT_WIDTH), lambda i, n: (i, n, 0)),
                  kv_spec(kb, prev), kv_spec(kb, lambda n: n), kv_spec(kb, nxt),
                  kv_spec(vb, prev), kv_spec(vb, lambda n: n), kv_spec(vb, nxt),
                  pl.BlockSpec((None, l, KV_WIDTH), lambda i, n: (i, 0, kxb)),
                  pl.BlockSpec((None, l, KV_WIDTH), lambda i, n: (i, 0, vxb)),
                  table, table, table, table, table, table],
        out_specs=pl.BlockSpec((None, blk, ATT_WIDTH), lambda i, n: (i, n, 0)),
        out_shape=jax.ShapeDtypeStruct((b, s, ATT_WIDTH), BF16),
        compiler_params=_params("parallel", "parallel"),
        name="latent_attn",
    )(sink, q, kvx, kvx, kvx, kvx, kvx, kvx, kvx_ctx, kvx_ctx, *q_tables, *k_tables)


def _context_attn_kernel(sink_ref, q_ref, k_ref, v_ref, o_ref):
    h = pl.program_id(2)
    blk = q_ref.shape[0]
    q = jnp.concatenate([q_ref[:, r * HEAD_DIM:(r + 1) * HEAD_DIM] for r in range(REP)], axis=0)
    s = lax.dot_general(q, k_ref[...], (((1,), (1,)), ((), ())), preferred_element_type=F32)
    o = _softmax_pv(s * HEAD_DIM ** -0.5, _sink_column(sink_ref, h, blk), v_ref[...])
    for r in range(REP):
        o_ref[:, r * HEAD_DIM:(r + 1) * HEAD_DIM] = o[r * blk:(r + 1) * blk].astype(o_ref.dtype)


def context_attention(q, kvx, sink):
    b, l, _ = q.shape
    blk = min(ATT_BLOCK, l)
    kb, vb, qb = 0, KVX_V // HEAD_DIM, 0
    return pl.pallas_call(
        _context_attn_kernel,
        grid=(b, l // blk, N_KV_HEADS),
        in_specs=[pl.BlockSpec(memory_space=pltpu.SMEM),
                  pl.BlockSpec((None, blk, REP * HEAD_DIM), lambda i, n, h: (i, n, qb + h)),
                  pl.BlockSpec((None, l, HEAD_DIM), lambda i, n, h: (i, 0, kb + h)),
                  pl.BlockSpec((None, l, HEAD_DIM), lambda i, n, h: (i, 0, vb + h))],
        out_specs=pl.BlockSpec((None, blk, REP * HEAD_DIM), lambda i, n, h: (i, n, h)),
        out_shape=jax.ShapeDtypeStruct((b, l, ATT_WIDTH), BF16),
        compiler_params=_params("parallel", "parallel", "arbitrary"),
        name="context_attn",
    )(sink, q, kvx, kvx)


def _split3(x):
    hi = x.astype(BF16).astype(F32)
    r1 = x - hi
    mid = r1.astype(BF16).astype(F32)
    lo = (r1 - mid).astype(BF16).astype(F32)
    return hi, mid, lo


def _softplus(x):
    return jnp.maximum(x, 0.0) + jnp.log1p(jnp.exp(-jnp.abs(x)))


def _ssd_kernel(*refs, nc, zero_init, with_output):
    refs = list(refs)
    x_ref, b_ref, c_ref, dtt_ref, bias_ref, alog_ref = refs[:6]
    del refs[:6]
    z_ref, dskip_ref = (refs.pop(0), refs.pop(0)) if with_output else (None, None)
    init_ref = None if zero_init else refs.pop(0)
    u_ref, ssq_ref = (refs.pop(0), refs.pop(0)) if with_output else (None, None)
    fin_ref, state_ref, rhs_ref, decay_ref, exit_ref, entry_ref = refs[:6]
    y_ref = refs[6] if with_output else None
    q = SSD_CHUNK
    e_heads = HEADS_PER_GROUP
    wide = e_heads * q
    ii = lax.broadcasted_iota(jnp.int32, (q, q), 0)
    jj = lax.broadcasted_iota(jnp.int32, (q, q), 1)
    eye = (ii == jj).astype(BF16)
    left_b = (jj < SSD_HEAD_DIM).astype(BF16)
    right_b = (jj >= SSD_HEAD_DIM).astype(BF16)
    r32 = lax.broadcasted_iota(jnp.int32, (32, wide), 0)
    c32 = lax.broadcasted_iota(jnp.int32, (32, wide), 1) // q
    head_rows = ((r32 < 3 * e_heads) & ((r32 & (e_heads - 1)) == c32)).astype(F32)
    r8 = lax.broadcasted_iota(jnp.int32, (e_heads, wide), 0)
    c8 = lax.broadcasted_iota(jnp.int32, (e_heads, wide), 1) // q
    own_block = (r8 == c8).astype(F32)
    re = lax.broadcasted_iota(jnp.int32, (q, GROUP_WIDTH), 0)
    ce = lax.broadcasted_iota(jnp.int32, (q, GROUP_WIDTH), 1) // SSD_HEAD_DIM
    expand = ((re >= 64) & (re < 64 + 3 * e_heads) & ((re & (e_heads - 1)) == ce)).astype(BF16)
    zeros8 = jnp.zeros((e_heads, q), F32)
    ones24 = jnp.ones((3 * e_heads, q), F32)

    head_params = []
    for d in range(2):
        lo, hi = d * e_heads, (d + 1) * e_heads
        visible = (jj >= ii) if d else (jj <= ii)
        cum_mat = ((jj <= ii) if d else (jj >= ii)).astype(BF16)
        head_params.append((bias_ref[lo:hi, :], -jnp.exp(alog_ref[lo:hi, :]), cum_mat))

        @pl.when(pl.program_id(1) == 0)
        def _(d=d, visible=visible):
            rhs_ref[d, 0:q, :] = jnp.tile(jnp.where(visible, 0.0, NEG_BIG),
                                          (1, e_heads)).astype(BF16)
            rhs_ref[d, q:q + 32, :] = head_rows.astype(BF16)
            rhs_ref[d, q + 32:2 * q, :] = jnp.zeros((q - 32, wide), BF16)

        if zero_init:
            state_ref[d] = jnp.zeros(state_ref.shape[1:], F32)
        else:
            state_ref[d] = init_ref[d]

    def chunk_rows(d, step):
        c = (nc - 1 - step) if d else step
        return pl.ds(c * q if isinstance(c, int) else pl.multiple_of(c * q, q), q)

    def prepare_cumsum(d, step):
        lo, hi = d * e_heads, (d + 1) * e_heads
        bias_t, aneg_t, cum_mat = head_params[d]
        rows = chunk_rows(d, step)
        dt = _softplus(dtt_ref[lo:hi, rows] + bias_t)
        da = jnp.concatenate(_split3(dt * aneg_t), axis=0).astype(BF16)
        return dict(d=d, rows=rows, dt=dt, cs=jnp.dot(da, cum_mat, preferred_element_type=F32))

    def prepare_matmuls(v):
        d, dt, cs = v["d"], v["dt"], v["cs"]
        last = 0 if d else q - 1
        bc = b_ref[v["rows"], :]
        cc = c_ref[v["rows"], :]
        a2 = (cs[0:e_heads] + cs[e_heads:2 * e_heads] + cs[2 * e_heads:]) * LOG2E
        ea = jnp.exp2(a2)
        w_exit = jnp.exp2(a2[:, last:last + 1] - a2) * dt
        g = jnp.maximum(jnp.log2(dt), LOG2_DT_FLOOR) - a2
        rhs_ref[d, q + 32:q + 64, :] = jnp.concatenate(
            [jnp.tile(part, (1, e_heads)) * own_block for part in _split3(g)]
            + [jnp.zeros((e_heads, wide), F32)], axis=0).astype(BF16)
        split_t = jnp.concatenate(list(_split3(a2)) + [zeros8, ones24, zeros8]
                                  + list(_split3(ea)) + [jnp.zeros((q - 88, q), F32)], axis=0)
        split_n = split_t.T.astype(BF16)
        v["seg"] = jnp.dot(jnp.concatenate([eye, split_n], axis=1), rhs_ref[d],
                           preferred_element_type=F32)
        v["ea_wide"] = jnp.dot(split_n, expand, preferred_element_type=F32)
        v["cb"] = lax.dot_general(cc, bc, (((1,), (1,)), ((), ())), preferred_element_type=F32)
        b_t = bc.astype(F32).T
        v["b_exit"] = jnp.concatenate([b_t * w_exit[e:e + 1, :] for e in range(e_heads)],
                                      axis=1).astype(BF16)

    def prepare_store(v, slot):
        d = v["d"]
        decay_ref[slot, d] = (jnp.tile(v["cb"], (1, e_heads)) * jnp.exp2(v["seg"])).astype(BF16)
        exit_ref[slot, d] = v["b_exit"]
        entry_ref[slot, d] = v["ea_wide"]

    def emit_entry(d, step):
        rows = chunk_rows(d, step)
        state = state_ref[d]
        w = dict(d=d, rows=rows, state=state)
        if with_output:
            w["y_off"] = jnp.dot(c_ref[rows, :], state.astype(BF16), preferred_element_type=F32)
        return w

    def emit_outputs(w, slot, second_visit):
        d, rows = w["d"], w["rows"]
        last = 0 if d else q - 1
        xc = x_ref[rows, :]
        ssq = jnp.zeros((q, LANES), F32)
        for p in range(e_heads // 2):
            cols = slice(p * LANES, (p + 1) * LANES)
            two = slice(2 * p * q, 2 * (p + 1) * q)
            xp = xc[:, cols]
            x_pair = jnp.concatenate([xp * left_b, xp * right_b], axis=0)
            entry = entry_ref[slot, d, :, cols]
            if not with_output:
                new = jnp.dot(exit_ref[slot, d, :, two], x_pair, preferred_element_type=F32)
                state_ref[d, :, cols] = w["state"][:, cols] * entry[last:last + 1, :] + new
                continue
            lhs = jnp.concatenate([decay_ref[slot, d, :, two], exit_ref[slot, d, :, two]], axis=0)
            out = jnp.dot(lhs, x_pair, preferred_element_type=F32)
            state_ref[d, :, cols] = w["state"][:, cols] * entry[last:last + 1, :] + out[q:]
            y = out[:q] + w["y_off"][:, cols] * entry
            if not second_visit:
                y_ref[rows, cols] = y
                continue
            z = z_ref[rows, cols].astype(F32)
            y = y_ref[rows, cols] + y + xp.astype(F32) * dskip_ref[:, cols]
            gated = y * (z * jax.nn.sigmoid(z))
            u_ref[rows, cols] = gated.astype(u_ref.dtype)
            ssq = ssq + gated * gated
        if with_output and second_visit:
            ssq_ref[rows, :] = ssq

    def prepare_all(step, slot):
        chains = [prepare_cumsum(d, step) for d in range(2)]
        for v in chains:
            prepare_matmuls(v)
        for v in chains:
            prepare_store(v, slot)

    def body(second_visits, final=False):
        def run(it, carry):
            for half in range(2):
                step = 2 * it + half
                look_ahead = not (final and half)
                chains = [prepare_cumsum(d, step + 1) for d in range(2)] if look_ahead else []
                entries = [emit_entry(d, step) for d in range(2)]
                for n, w in enumerate(entries):
                    emit_outputs(w, half, second_visits[half])
                    if look_ahead:
                        prepare_matmuls(chains[n])
                for v in chains:
                    prepare_store(v, 1 - half)
            return carry
        return run

    prepare_all(0, 0)
    n_it = nc // 2
    if nc == 2:
        body((False, True), final=True)(0, 0)
    else:
        lax.fori_loop(0, n_it // 2, body((False, False)), 0)
        lax.fori_loop(n_it // 2, n_it - 1, body((True, True)), 0)
        body((True, True), final=True)(n_it - 1, 0)
    for d in range(2):
        fin_ref[d] = state_ref[d]


def ssd_scan(xbc, dt_t, bias_t, alog_t, init, gate=None):
    b, t, _ = xbc.shape
    nc = t // SSD_CHUNK
    assert nc == 2 or nc % 4 == 0
    wide = HEADS_PER_GROUP * SSD_CHUNK
    g = SSD_GROUPS
    e2 = 2 * HEADS_PER_GROUP
    bm0 = D_INNER // D_STATE
    cm0 = bm0 + g
    state_shape = (2, D_STATE, GROUP_WIDTH)
    group_block = pl.BlockSpec((None, t, GROUP_WIDTH), lambda i, j: (i, 0, j))
    state_block = pl.BlockSpec((None, None) + state_shape, lambda i, j: (i, j, 0, 0, 0))
    in_specs = [group_block,
                pl.BlockSpec((None, t, D_STATE), lambda i, j: (i, 0, bm0 + j)),
                pl.BlockSpec((None, t, D_STATE), lambda i, j: (i, 0, cm0 + j)),
                pl.BlockSpec((e2, t), lambda i, j: (j, i)),
                pl.BlockSpec((None, e2, 1), lambda i, j: (j, 0, 0)),
                pl.BlockSpec((None, e2, 1), lambda i, j: (j, 0, 0))]
    args = [xbc, xbc, xbc, dt_t, bias_t, alog_t]
    out_specs, out_shape = [state_block], [jax.ShapeDtypeStruct((b, g) + state_shape, F32)]
    scratch = [pltpu.VMEM(state_shape, F32),
               pltpu.VMEM((2, 2 * SSD_CHUNK, wide), BF16),
               pltpu.VMEM((2, 2, SSD_CHUNK, wide), BF16),
               pltpu.VMEM((2, 2, D_STATE, wide), BF16),
               pltpu.VMEM((2, 2, SSD_CHUNK, GROUP_WIDTH), F32)]
    if gate is not None:
        p, dskip = gate
        in_specs += [group_block,
                     pl.BlockSpec((None, 1, GROUP_WIDTH), lambda i, j: (j, 0, 0))]
        args += [p, dskip]
        out_specs = [group_block, pl.BlockSpec((None, t, LANES), lambda i, j: (i, 0, j))] + out_specs
        out_shape = [jax.ShapeDtypeStruct((b, t, D_INNER), BF16),
                     jax.ShapeDtypeStruct((b, t, g * LANES), F32)] + out_shape
        scratch.append(pltpu.VMEM((t, GROUP_WIDTH), F32))
    if init is not None:
        in_specs.append(state_block)
        args.append(init)
    outs = pl.pallas_call(
        functools.partial(_ssd_kernel, nc=nc, zero_init=init is None, with_output=gate is not None),
        grid=(b, g),
        in_specs=in_specs,
        out_specs=out_specs,
        out_shape=out_shape,
        scratch_shapes=scratch,
        compiler_params=_params("parallel", "arbitrary"),
        name="ssd_scan",
    )(*args)
    return tuple(outs) if gate is not None else (None, None, outs[0])


def _rope_tables(seq):
    t = jnp.arange(seq)
    row = (t // GRID_W).astype(F32)
    col = (t % GRID_W).astype(F32)
    axis_dim = HEAD_DIM // 2
    inv_freq = ROPE_THETA ** (-jnp.arange(0, axis_dim, 2, dtype=F32) / axis_dim)
    ang_r = row[:, None] * inv_freq[None]
    ang_c = col[:, None] * inv_freq[None]
    cos_r, sin_r, cos_c, sin_c = jnp.cos(ang_r), jnp.sin(ang_r), jnp.cos(ang_c), jnp.sin(ang_c)
    zero = jnp.zeros_like(sin_r)
    cos = jnp.concatenate([cos_r, cos_r, cos_c, cos_c], axis=-1)
    sin_next = jnp.concatenate([-sin_r, zero, -sin_c, zero], axis=-1)
    sin_prev = jnp.concatenate([zero, sin_r, zero, sin_c], axis=-1)
    return cos, sin_next, sin_prev


def _group_major(v):
    return v.reshape(2, SSD_GROUPS, HEADS_PER_GROUP).transpose(1, 0, 2).reshape(
        SSD_GROUPS, 2 * HEADS_PER_GROUP)


def _project(h2, w, shape, name):
    n_out = _weight_array(w).shape[-1]
    return matmul([(h2, w)], [], lambda parts, ex: parts[0], n_out, BF16, tm=2048, tn=1024,
                  name=name).reshape(shape + (n_out,))


def _mixer_side(h, dt_t, w_kvx, w_q, w_zg, conv_w, conv_b, ssd_params, init, dskip):
    b, t, d = h.shape
    h2 = h.reshape(b * t, d)
    kvx = _project(h2, w_kvx, (b, t), "in_proj_kvx")
    q = zg = gate = None
    if dskip is not None:
        q = _project(h2, w_q, (b, t), "in_proj_q")
        zg = _project(h2, w_zg, (b, t), "in_proj_zg")
        gate = (zg, dskip)
    xbc = conv_silu(kvx, conv_w, conv_b)
    u, ssq, fin = ssd_scan(xbc, dt_t, *ssd_params, init, gate)
    return (kvx, q, zg), u, ssq, fin


def _merge_out(att, u, ssq, zg, w_o_attn, w_o_ssd_g, w_out, x, mod, mod_row):
    b, t, d = x.shape
    m = b * t
    tm, tn = min(512, t), 1024
    p2 = zg.reshape(m, zg.shape[-1])
    ga0, gs0 = ZG_GATE_A // tn, ZG_GATE_S // tn
    n_part = ssq.shape[-1]

    def merge(parts, ex):
        ga, gs, sq = ex
        inv_rms = lax.rsqrt(jnp.sum(sq, axis=-1, keepdims=True) * (1.0 / D_INNER) + EPS)
        return (jax.nn.sigmoid(ga.astype(F32)) * parts[0]
                + jax.nn.sigmoid(gs.astype(F32)) * (inv_rms * parts[1]))

    merged = matmul([(att.reshape(m, ATT_WIDTH), w_o_attn), (u.reshape(m, D_INNER), w_o_ssd_g)],
                    [(p2, (tm, tn), lambda i, j: (i, ga0 + j)),
                     (p2, (tm, tn), lambda i, j: (i, gs0 + j)),
                     (ssq.reshape(m, n_part), (tm, n_part), lambda i, j: (i, 0))],
                    merge, d, BF16, tm=tm, tn=tn, w_stationary=True, name="merge")
    return _residual_matmul(merged, w_out, x, mod, mod_row, 2, name="out_proj")


def _residual_matmul(a, w, x, mod, mod_row, gate_idx, name):
    b, t, d = x.shape
    ktiled = a.shape[1] > 4096
    tm, tn = (min(1024, t), 1024) if ktiled else (min(512, t), d)
    per_seq = t // tm
    nj = d // tn
    extras = [(x.reshape(b * t, d), (tm, tn), lambda i, j: (i, j)),
              (mod, (None, 1, tn), lambda i, j: (mod_row(i // per_seq), 0, gate_idx * nj + j))]
    res = lambda parts, ex: ex[0] + ex[1] * parts[0]
    if ktiled:
        out = matmul_ktiled(a, w, extras, res, F32, tm=tm, tn=tn, tk=2048, name=name)
    else:
        out = matmul([(a, w)], extras, res, d, F32, tm=tm, tn=tn, name=name)
    return out.reshape(b, t, d)


def _mlp(x, g, w1, w2, mod, mod_row):
    b, t, d = x.shape
    h = norm_modulate(x, g, mod, mod_row, 3).reshape(b * t, d)
    hid = matmul([(h, w1)], [], lambda parts, ex: jnp.square(jnp.maximum(parts[0], 0.0)), D_FF,
                 BF16, tm=2048, tn=1024, name="ff1")
    return _residual_matmul(hid, w2, x, mod, mod_row, 5, name="ff2")


def kernel(x, c, ctx, c_ctx, w_ada, b_ada, g_norm1, g_norm2, w_in, attn_sink, conv_w, conv_b,
           dt_bias, a_log, d_skip, g_ssd, w_o_attn, w_o_ssd, w_out, w_ff1, w_ff2, g_final):
    depth = w_in.shape[0]
    batch, seq, d = x.shape
    q_scale = LOG2E * HEAD_DIM ** -0.5
    k_tables = _rope_tables(seq)
    q_tables = tuple(tb * q_scale for tb in k_tables)

    n_rows = -(-(batch + 1) // 8) * 8
    cond = jnp.zeros((n_rows, d), F32).at[:batch].set(c).at[batch].set(c_ctx)
    mod = adaln_modulation(cond, w_ada, b_ada[:, None, :]).reshape(depth * n_rows, 1, N_MOD * d)
    w_kvx_all = w_in[:, :, :REF_COL_DT].astype(BF16)
    w_q_all = w_in[:, :, REF_COL_Q:REF_COL_Z].astype(BF16)
    w_zg_all = w_in[:, :, REF_COL_Z:].astype(BF16)
    wa_all, wo_all = w_o_attn.astype(BF16), w_out.astype(BF16)
    ws_all = (g_ssd[:, :, None] * w_o_ssd).astype(BF16)
    w1_all, w2_all = w_ff1.astype(BF16), w_ff2.astype(BF16)

    h_ctx = ctx
    for i in range(depth):
        ctx_out = i < depth - 1
        latent_row = lambda b, i=i: i * n_rows + b
        ctx_row = lambda b, i=i: i * n_rows + batch
        w_kvx, w_q, w_zg = (w_kvx_all, i), (w_q_all, i), (w_zg_all, i)
        wa, ws, wo, w1, w2 = (wa_all, i), (ws_all, i), (wo_all, i), (w1_all, i), (w2_all, i)
        w_dt_t = w_in[i][:, REF_COL_DT:REF_COL_Q].reshape(
            d, 2, SSD_GROUPS, HEADS_PER_GROUP).transpose(2, 1, 3, 0).reshape(
                2 * SSD_HEADS, d).astype(BF16)
        ssd_params = (_group_major(dt_bias[i])[:, :, None], _group_major(a_log[i])[:, :, None])
        dskip = jnp.repeat(d_skip[i][0] + d_skip[i][1], SSD_HEAD_DIM).reshape(
            SSD_GROUPS, 1, GROUP_WIDTH)
        sink = attn_sink[i]
        g1, g2 = g_norm1[i][None], g_norm2[i][None]

        hc, dt_c = norm_modulate(h_ctx, g1, mod, ctx_row, 0, w_dt_t)
        (kvx_c, q_c, zg_c), u_c, ssq_c, fin_c = _mixer_side(
            hc, dt_c, w_kvx, w_q if ctx_out else None, w_zg if ctx_out else None, conv_w[i],
            conv_b[i][None], ssd_params, None, dskip if ctx_out else None)
        hx, dt_x = norm_modulate(x, g1, mod, latent_row, 0, w_dt_t)
        (kvx_x, q_x, zg_x), u_x, ssq_x, _ = _mixer_side(
            hx, dt_x, w_kvx, w_q, w_zg, conv_w[i], conv_b[i][None], ssd_params, fin_c, dskip)
        att_x = latent_attention(q_x, kvx_x, kvx_c, sink, q_tables, k_tables)
        x = _merge_out(att_x, u_x, ssq_x, zg_x, wa, ws, wo, x, mod, latent_row)
        x = _mlp(x, g2, w1, w2, mod, latent_row)
        if ctx_out:
            att_c = context_attention(q_c, kvx_c, sink)
            h_ctx = _merge_out(att_c, u_c, ssq_c, zg_c, wa, ws, wo, h_ctx, mod, ctx_row)
            h_ctx = _mlp(h_ctx, g2, w1, w2, mod, ctx_row)
    return final_norm(x, g_final[None])
```

```python
import functools

import jax
import jax.numpy as jnp
from jax import lax
from jax.experimental import pallas as pl
from jax.experimental.pallas import tpu as pltpu

F32 = jnp.float32
BF16 = jnp.bfloat16

D_MODEL = 2048
GRID_W = 64
N_HEADS = 16
N_KV_HEADS = 4
HEAD_DIM = 128
REP = N_HEADS // N_KV_HEADS
ATT_WIDTH = N_HEADS * HEAD_DIM
KV_WIDTH = N_KV_HEADS * HEAD_DIM
WINDOW = 128
ATT_BLOCK = 128
ROPE_THETA = 10000.0
D_INNER = 2 * D_MODEL
SSD_HEAD_DIM = 64
SSD_HEADS = D_INNER // SSD_HEAD_DIM
SSD_GROUPS = 8
HEADS_PER_GROUP = SSD_HEADS // SSD_GROUPS
D_STATE = 128
CONV_K = 5
SSD_CHUNK = 128
GROUP_WIDTH = HEADS_PER_GROUP * SSD_HEAD_DIM
CONV_CH = D_INNER + 2 * SSD_GROUPS * D_STATE
D_FF = 4 * D_MODEL
N_MOD = 6
EPS = 1e-6
LOG2E = 1.4426950408889634
NEG_BIG = -1e30
LOG2_DT_FLOOR = -1e4

REF_COL_XBC = 2 * KV_WIDTH
REF_COL_DT = REF_COL_XBC + CONV_CH
REF_COL_Q = REF_COL_DT + 2 * SSD_HEADS
REF_COL_Z = REF_COL_Q + ATT_WIDTH
REF_COL_GATE = REF_COL_Z + D_INNER

KVX_V = KV_WIDTH
KVX_XBC = 2 * KV_WIDTH
ZG_GATE_A = D_INNER
ZG_GATE_S = D_INNER + D_MODEL

LANES = 128
VMEM_LIMIT = 56 * 1024 * 1024


def _params(*sem):
    return pltpu.CompilerParams(dimension_semantics=sem, vmem_limit_bytes=VMEM_LIMIT)


def _mm_kernel(*refs, n_pairs, n_extra, epilogue):
    a_refs = refs[:n_pairs]
    w_refs = refs[n_pairs:2 * n_pairs]
    extra = refs[2 * n_pairs:2 * n_pairs + n_extra]
    o_ref = refs[2 * n_pairs + n_extra]
    parts = [jnp.dot(a[...], w[...], preferred_element_type=F32) for a, w in zip(a_refs, w_refs)]
    o_ref[...] = epilogue(parts, [e[...] for e in extra]).astype(o_ref.dtype)


def _weight_array(w):
    return w[0] if isinstance(w, tuple) else w


def _weight_spec(w, tn, index_map, at):
    if not isinstance(w, tuple):
        return pl.BlockSpec((w.shape[0], tn), at(index_map))
    stack, layer = w
    return pl.BlockSpec((None, stack.shape[1], tn),
                        at(lambda *ij: (layer,) + tuple(index_map(*ij))))


def matmul(pairs, extras, epilogue, n_out, out_dtype, tm, tn, w_stationary=False, name="mm"):
    m = pairs[0][0].shape[0]
    tm = min(tm, m)
    assert m % tm == 0 and n_out % tn == 0
    if w_stationary:
        grid = (n_out // tn, m // tm)
        at = lambda im: (lambda j, i: im(i, j))
    else:
        grid = (m // tm, n_out // tn)
        at = lambda im: im
    in_specs = [pl.BlockSpec((tm, a.shape[1]), at(lambda i, j: (i, 0))) for a, _ in pairs]
    in_specs += [_weight_spec(w, tn, lambda i, j: (0, j), at) for _, w in pairs]
    in_specs += [pl.BlockSpec(bs, at(im)) for _, bs, im in extras]
    pairs = [(a, _weight_array(w)) for a, w in pairs]
    kern = functools.partial(_mm_kernel, n_pairs=len(pairs), n_extra=len(extras), epilogue=epilogue)
    return pl.pallas_call(
        kern,
        grid=grid,
        in_specs=in_specs,
        out_specs=pl.BlockSpec((tm, tn), at(lambda i, j: (i, j))),
        out_shape=jax.ShapeDtypeStruct((m, n_out), out_dtype),
        compiler_params=_params("parallel", "arbitrary"),
        name=name,
    )(*[a for a, _ in pairs], *[w for _, w in pairs], *[e for e, _, _ in extras])


def _adaln_kernel(c_ref, w_ref, b_ref, o_ref):
    c = c_ref[...]
    silu_c = c * jax.nn.sigmoid(c)
    o_ref[...] = jnp.dot(silu_c, w_ref[...], preferred_element_type=F32,
                         precision=lax.Precision.HIGHEST) + b_ref[...]


def adaln_modulation(cond, w, b, tn=512):
    r, d = cond.shape
    depth, _, n = w.shape
    return pl.pallas_call(
        _adaln_kernel,
        grid=(depth, n // tn),
        in_specs=[pl.BlockSpec((r, d), lambda l, j: (0, 0)),
                  pl.BlockSpec((None, d, tn), lambda l, j: (l, 0, j)),
                  pl.BlockSpec((None, 1, tn), lambda l, j: (l, 0, j))],
        out_specs=pl.BlockSpec((None, r, tn), lambda l, j: (l, 0, j)),
        out_shape=jax.ShapeDtypeStruct((depth, r, n), F32),
        compiler_params=_params("parallel", "arbitrary"),
        name="adaln",
    )(cond, w, b)


def _norm_mod_kernel(x_ref, g_ref, shift_ref, scale_ref, *rest):
    x = x_ref[...]
    y = x * lax.rsqrt(jnp.mean(x * x, axis=-1, keepdims=True) + EPS) * g_ref[...]
    h = (y * (1.0 + scale_ref[...]) + shift_ref[...]).astype(BF16)
    if len(rest) == 1:
        rest[0][...] = h
        return
    wt_ref, o_ref, dt_ref = rest
    o_ref[...] = h
    dt_ref[...] = lax.dot_general(wt_ref[...], h, (((1,), (1,)), ((), ())),
                                  preferred_element_type=F32)


def norm_modulate(x, g, mod, mod_row, shift_idx, w_dt_t=None, ts=1024):
    b, t, d = x.shape
    ts = min(ts, t)
    per_seq = t // ts
    in_specs = [pl.BlockSpec((None, ts, d), lambda i, j: (i, j, 0)),
                pl.BlockSpec((1, d), lambda i, j: (0, 0)),
                pl.BlockSpec((None, 1, d), lambda i, j: (mod_row(i), 0, shift_idx)),
                pl.BlockSpec((None, 1, d), lambda i, j: (mod_row(i), 0, shift_idx + 1))]
    out_specs = pl.BlockSpec((None, ts, d), lambda i, j: (i, j, 0))
    out_shape = jax.ShapeDtypeStruct((b, t, d), BF16)
    args = [x, g, mod, mod]
    if w_dt_t is not None:
        in_specs.append(pl.BlockSpec((LANES, d), lambda i, j: (0, 0)))
        out_specs = [out_specs, pl.BlockSpec((LANES, ts), lambda i, j: (0, i * per_seq + j))]
        out_shape = [out_shape, jax.ShapeDtypeStruct((LANES, b * t), F32)]
        args.append(w_dt_t)
    return pl.pallas_call(
        _norm_mod_kernel,
        grid=(b, t // ts),
        in_specs=in_specs,
        out_specs=out_specs,
        out_shape=out_shape,
        compiler_params=_params("parallel", "parallel"),
        name="norm_mod",
    )(*args)


def _final_norm_kernel(x_ref, g_ref, o_ref):
    x = x_ref[...]
    o_ref[...] = x * lax.rsqrt(jnp.mean(x * x, axis=-1, keepdims=True) + EPS) * g_ref[...]


def final_norm(x, g, ts=1024):
    b, t, d = x.shape
    ts = min(ts, t)
    return pl.pallas_call(
        _final_norm_kernel,
        grid=(b, t // ts),
        in_specs=[pl.BlockSpec((None, ts, d), lambda i, j: (i, j, 0)),
                  pl.BlockSpec((1, d), lambda i, j: (0, 0))],
        out_specs=pl.BlockSpec((None, ts, d), lambda i, j: (i, j, 0)),
        out_shape=jax.ShapeDtypeStruct((b, t, d), F32),
        compiler_params=_params("parallel", "parallel"),
        name="final_norm",
    )(x, g)


CONV_ROWS = 128
CONV_HALO = 16
CONV_UNROLL = 16


def _conv_silu_kernel(u_ref, w_ref, b_ref, o_ref, ext_ref, *, unroll):
    t, tc = u_ref.shape
    pad = CONV_K // 2
    span = CONV_ROWS + 2 * CONV_HALO
    halo = jnp.zeros((CONV_HALO, tc), BF16)
    ext_ref[0:CONV_HALO, :] = halo
    ext_ref[CONV_HALO + t:, :] = halo
    ext_ref[CONV_HALO:CONV_HALO + t, :] = u_ref[...]
    taps = [k - pad for k in range(CONV_K) if k != pad]
    ri = lax.broadcasted_iota(jnp.int32, (len(taps) * CONV_ROWS, span), 0)
    ci = lax.broadcasted_iota(jnp.int32, (len(taps) * CONV_ROWS, span), 1)
    tap_off = jnp.zeros_like(ri)
    for n, off in enumerate(taps):
        tap_off = jnp.where(ri // CONV_ROWS == n, off, tap_off)
    shift_mat = (ci == (ri % CONV_ROWS) + CONV_HALO + tap_off).astype(BF16)
    bias = b_ref[...]
    w = [w_ref[k:k + 1, :] for k in range(CONV_K)]

    def body(it, carry):
        windows, shifted = [], []
        for n in range(unroll):
            r0 = pl.multiple_of((it * unroll + n) * CONV_ROWS, CONV_ROWS)
            win = ext_ref[pl.ds(r0, span), :]
            windows.append((r0, win))
            shifted.append(jnp.dot(shift_mat, win, preferred_element_type=F32))
        for (r0, win), sh in zip(windows, shifted):
            acc = bias + w[pad] * win[CONV_HALO:CONV_HALO + CONV_ROWS].astype(F32)
            for n, off in enumerate(taps):
                acc = acc + w[pad + off] * sh[n * CONV_ROWS:(n + 1) * CONV_ROWS]
            o_ref[pl.ds(r0, CONV_ROWS), :] = (acc * jax.nn.sigmoid(acc)).astype(o_ref.dtype)
        return carry

    lax.fori_loop(0, t // (CONV_ROWS * unroll), body, 0)


def conv_silu(p, conv_w, conv_b):
    b, t, _ = p.shape
    tc = 1024
    unroll = min(CONV_UNROLL, t // CONV_ROWS)
    assert t % (CONV_ROWS * unroll) == 0
    c0 = KVX_XBC // tc
    return pl.pallas_call(
        functools.partial(_conv_silu_kernel, unroll=unroll),
        grid=(b, CONV_CH // tc),
        in_specs=[pl.BlockSpec((None, t, tc), lambda i, j: (i, 0, c0 + j)),
                  pl.BlockSpec((CONV_K, tc), lambda i, j: (0, j)),
                  pl.BlockSpec((1, tc), lambda i, j: (0, j))],
        out_specs=pl.BlockSpec((None, t, tc), lambda i, j: (i, 0, j)),
        out_shape=jax.ShapeDtypeStruct((b, t, CONV_CH), BF16),
        scratch_shapes=[pltpu.VMEM((t + 2 * CONV_HALO, tc), BF16)],
        compiler_params=_params("parallel", "parallel"),
        name="conv_silu",
    )(p, conv_w, conv_b)


def _rope(u, cos, sin_next, sin_prev):
    quarter = HEAD_DIM // 4
    return (u * cos + pltpu.roll(u, HEAD_DIM - quarter, axis=1) * sin_next
            + pltpu.roll(u, quarter, axis=1) * sin_prev)


def _softmax_pv(s, sink_col, v):
    m = jnp.maximum(jnp.max(s, axis=-1, keepdims=True), sink_col)
    e = jnp.exp(s - m)
    denom = jnp.sum(e, axis=-1, keepdims=True) + jnp.exp(sink_col - m)
    o = jnp.dot(e.astype(BF16), v, preferred_element_type=F32)
    return o / denom


def _sink_column(sink_ref, h, rows):
    blk = lax.broadcasted_iota(jnp.int32, (REP * rows, 1), 0) // rows
    col = jnp.zeros((REP * rows, 1), F32)
    for r in range(REP):
        col = jnp.where(blk == r, sink_ref[h * REP + r], col)
    return col


def _latent_attn_kernel(sink_ref, q_ref, kp_ref, kc_ref, kn_ref, vp_ref, vc_ref, vn_ref,
                        kx_ref, vx_ref, qcos_ref, qsn_ref, qsp_ref, cos_ref, sn_ref, sp_ref, o_ref,
                        *, nb):
    n = pl.program_id(1)
    blk = ATT_BLOCK

    def rows(ref, i):
        return ref[pl.ds(pl.multiple_of(i * blk, blk), blk), :]

    n_prev = jnp.maximum(n - 1, 0)
    n_next = jnp.minimum(n + 1, nb - 1)
    k_tables = [[rows(ref, i) for ref in (cos_ref, sn_ref, sp_ref)] for i in (n_prev, n, n_next)]
    qcos, qsn, qsp = rows(qcos_ref, n), rows(qsn_ref, n), rows(qsp_ref, n)
    qi = lax.broadcasted_iota(jnp.int32, (blk, blk), 0)
    kj = lax.broadcasted_iota(jnp.int32, (blk, blk), 1)
    see_prev = kj >= qi + (n == 0).astype(jnp.int32) * blk
    see_next = kj <= qi - (n == nb - 1).astype(jnp.int32) * blk
    def head_cols(h):
        return [slice((h * REP + r) * HEAD_DIM, (h * REP + r + 1) * HEAD_DIM) for r in range(REP)]

    def scores(h):
        kv = slice(h * HEAD_DIM, (h + 1) * HEAD_DIM)
        keys = jnp.concatenate(
            [_rope(ref[:, kv].astype(F32), *tabs).astype(BF16)
             for ref, tabs in zip((kp_ref, kc_ref, kn_ref), k_tables)] + [kx_ref[:, kv]], axis=0)
        return [lax.dot_general(_rope(q_ref[:, cols].astype(F32), qcos, qsn, qsp).astype(BF16),
                                keys, (((1,), (1,)), ((), ())), preferred_element_type=F32)
                for cols in head_cols(h)]

    def softmax_terms(h, logits):
        probs, sink_terms = [], []
        for r, s in enumerate(logits):
            s = jnp.concatenate(
                [jnp.where(see_prev, s[:, :blk], -jnp.inf), s[:, blk:2 * blk],
                 jnp.where(see_next, s[:, 2 * blk:3 * blk], -jnp.inf), s[:, 3 * blk:]], axis=1)
            sink = sink_ref[h * REP + r] * LOG2E
            m = jnp.maximum(jnp.max(s, axis=-1, keepdims=True), sink)
            probs.append(jnp.exp2(s - m).astype(BF16))
            sink_terms.append(jnp.exp2(sink - m))
        return probs, sink_terms

    def outputs(h, probs, sink_terms):
        kv = slice(h * HEAD_DIM, (h + 1) * HEAD_DIM)
        vals = jnp.concatenate([vp_ref[:, kv], vc_ref[:, kv], vn_ref[:, kv], vx_ref[:, kv]], axis=0)
        vals = jnp.concatenate([vals, jnp.ones_like(vals)], axis=1)
        for cols, e, sink_term in zip(head_cols(h), probs, sink_terms):
            o = jnp.dot(e, vals, preferred_element_type=F32)
            o_ref[:, cols] = (o[:, :HEAD_DIM] / (o[:, HEAD_DIM:] + sink_term)).astype(o_ref.dtype)

    logits = scores(0)
    for h in range(N_KV_HEADS):
        ahead = scores(h + 1) if h + 1 < N_KV_HEADS else None
        outputs(h, *softmax_terms(h, logits))
        logits = ahead


def latent_attention(q, kvx, kvx_ctx, sink, q_tables, k_tables):
    b, s, _ = q.shape
    l = kvx_ctx.shape[1]
    nb = s // ATT_BLOCK
    blk = ATT_BLOCK
    kb = kxb = 0
    vb = vxb = KVX_V // KV_WIDTH
    prev = lambda n: jnp.maximum(n - 1, 0)
    nxt = lambda n: jnp.minimum(n + 1, nb - 1)
    kv_spec = lambda cb, f: pl.BlockSpec((None, blk, KV_WIDTH), lambda i, n: (i, f(n), cb))
    table = pl.BlockSpec((s, HEAD_DIM), lambda i, n: (0, 0))
    return pl.pallas_call(
        functools.partial(_latent_attn_kernel, nb=nb),
        grid=(b, nb),
        in_specs=[pl.BlockSpec(memory_space=pltpu.SMEM),
                  pl.BlockSpec((None, blk, ATT_WIDTH), lambda i, n: (i, n, 0)),
                  kv_spec(kb, prev), kv_spec(kb, lambda n: n), kv_spec(kb, nxt),
                  kv_spec(vb, prev), kv_spec(vb, lambda n: n), kv_spec(vb, nxt),
                  pl.BlockSpec((None, l, KV_WIDTH), lambda i, n: (i, 0, kxb)),
                  pl.BlockSpec((None, l, KV_WIDTH), lambda i, n: (i, 0, vxb)),
                  table, table, table, table, table, table],
        out_specs=pl.BlockSpec((None, blk, ATT_WIDTH), lambda i, n: (i, n, 0)),
        out_shape=jax.ShapeDtypeStruct((b, s, ATT_WIDTH), BF16),
        compiler_params=_params("parallel", "parallel"),
        name="latent_attn",
    )(sink, q, kvx, kvx, kvx, kvx, kvx, kvx, kvx_ctx, kvx_ctx, *q_tables, *k_tables)


def _context_attn_kernel(sink_ref, q_ref, k_ref, v_ref, o_ref):
    h = pl.program_id(2)
    blk = q_ref.shape[0]
    q = jnp.concatenate([q_ref[:, r * HEAD_DIM:(r + 1) * HEAD_DIM] for r in range(REP)], axis=0)
    s = lax.dot_general(q, k_ref[...], (((1,), (1,)), ((), ())), preferred_element_type=F32)
    o = _softmax_pv(s * HEAD_DIM ** -0.5, _sink_column(sink_ref, h, blk), v_ref[...])
    for r in range(REP):
        o_ref[:, r * HEAD_DIM:(r + 1) * HEAD_DIM] = o[r * blk:(r + 1) * blk].astype(o_ref.dtype)


def context_attention(q, kvx, sink):
    b, l, _ = q.shape
    blk = min(ATT_BLOCK, l)
    kb, vb, qb = 0, KVX_V // HEAD_DIM, 0
    return pl.pallas_call(
        _context_attn_kernel,
        grid=(b, l // blk, N_KV_HEADS),
        in_specs=[pl.BlockSpec(memory_space=pltpu.SMEM),
                  pl.BlockSpec((None, blk, REP * HEAD_DIM), lambda i, n, h: (i, n, qb + h)),
                  pl.BlockSpec((None, l, HEAD_DIM), lambda i, n, h: (i, 0, kb + h)),
                  pl.BlockSpec((None, l, HEAD_DIM), lambda i, n, h: (i, 0, vb + h))],
        out_specs=pl.BlockSpec((None, blk, REP * HEAD_DIM), lambda i, n, h: (i, n, h)),
        out_shape=jax.ShapeDtypeStruct((b, l, ATT_WIDTH), BF16),
        compiler_params=_params("parallel", "parallel", "arbitrary"),
        name="context_attn",
    )(sink, q, kvx, kvx)


def _split3(x):
    hi = x.astype(BF16).astype(F32)
    r1 = x - hi
    mid = r1.astype(BF16).astype(F32)
    lo = (r1 - mid).astype(BF16).astype(F32)
    return hi, mid, lo


def _softplus(x):
    return jnp.maximum(x, 0.0) + jnp.log1p(jnp.exp(-jnp.abs(x)))


def _ssd_kernel(*refs, nc, zero_init, with_output):
    refs = list(refs)
    x_ref, b_ref, c_ref, dtt_ref, bias_ref, alog_ref = refs[:6]
    del refs[:6]
    z_ref, dskip_ref = (refs.pop(0), refs.pop(0)) if with_output else (None, None)
    init_ref = None if zero_init else refs.pop(0)
    u_ref, ssq_ref = (refs.pop(0), refs.pop(0)) if with_output else (None, None)
    fin_ref, state_ref, rhs_ref, decay_ref, exit_ref, entry_ref = refs[:6]
    y_ref = refs[6] if with_output else None
    q = SSD_CHUNK
    e_heads = HEADS_PER_GROUP
    wide = e_heads * q
    ii = lax.broadcasted_iota(jnp.int32, (q, q), 0)
    jj = lax.broadcasted_iota(jnp.int32, (q, q), 1)
    eye = (ii == jj).astype(BF16)
    left_b = (jj < SSD_HEAD_DIM).astype(BF16)
    right_b = (jj >= SSD_HEAD_DIM).astype(BF16)
    r32 = lax.broadcasted_iota(jnp.int32, (32, wide), 0)
    c32 = lax.broadcasted_iota(jnp.int32, (32, wide), 1) // q
    head_rows = ((r32 < 3 * e_heads) & ((r32 & (e_heads - 1)) == c32)).astype(F32)
    r8 = lax.broadcasted_iota(jnp.int32, (e_heads, wide), 0)
    c8 = lax.broadcasted_iota(jnp.int32, (e_heads, wide), 1) // q
    own_block = (r8 == c8).astype(F32)
    re = lax.broadcasted_iota(jnp.int32, (q, GROUP_WIDTH), 0)
    ce = lax.broadcasted_iota(jnp.int32, (q, GROUP_WIDTH), 1) // SSD_HEAD_DIM
    expand = ((re >= 64) & (re < 64 + 3 * e_heads) & ((re & (e_heads - 1)) == ce)).astype(BF16)
    zeros8 = jnp.zeros((e_heads, q), F32)
    ones24 = jnp.ones((3 * e_heads, q), F32)

    head_params = []
    for d in range(2):
        lo, hi = d * e_heads, (d + 1) * e_heads
        visible = (jj >= ii) if d else (jj <= ii)
        cum_mat = ((jj <= ii) if d else (jj >= ii)).astype(BF16)
        head_params.append((bias_ref[lo:hi, :], -jnp.exp(alog_ref[lo:hi, :]), cum_mat))

        @pl.when(pl.program_id(1) == 0)
        def _(d=d, visible=visible):
            rhs_ref[d, 0:q, :] = jnp.tile(jnp.where(visible, 0.0, NEG_BIG),
                                          (1, e_heads)).astype(BF16)
            rhs_ref[d, q:q + 32, :] = head_rows.astype(BF16)
            rhs_ref[d, q + 32:2 * q, :] = jnp.zeros((q - 32, wide), BF16)

        if zero_init:
            state_ref[d] = jnp.zeros(state_ref.shape[1:], F32)
        else:
            state_ref[d] = init_ref[d]

    def chunk_rows(d, step):
        c = (nc - 1 - step) if d else step
        return pl.ds(c * q if isinstance(c, int) else pl.multiple_of(c * q, q), q)

    def prepare_cumsum(d, step):
        lo, hi = d * e_heads, (d + 1) * e_heads
        bias_t, aneg_t, cum_mat = head_params[d]
        rows = chunk_rows(d, step)
        dt = _softplus(dtt_ref[lo:hi, rows] + bias_t)
        da = jnp.concatenate(_split3(dt * aneg_t), axis=0).astype(BF16)
        return dict(d=d, rows=rows, dt=dt, cs=jnp.dot(da, cum_mat, preferred_element_type=F32))

    def prepare_matmuls(v):
        d, dt, cs = v["d"], v["dt"], v["cs"]
        last = 0 if d else q - 1
        bc = b_ref[v["rows"], :]
        cc = c_ref[v["rows"], :]
        a2 = (cs[0:e_heads] + cs[e_heads:2 * e_heads] + cs[2 * e_heads:]) * LOG2E
        ea = jnp.exp2(a2)
        w_exit = jnp.exp2(a2[:, last:last + 1] - a2) * dt
        g = jnp.maximum(jnp.log2(dt), LOG2_DT_FLOOR) - a2
        rhs_ref[d, q + 32:q + 64, :] = jnp.concatenate(
            [jnp.tile(part, (1, e_heads)) * own_block for part in _split3(g)]
            + [jnp.zeros((e_heads, wide), F32)], axis=0).astype(BF16)
        split_t = jnp.concatenate(list(_split3(a2)) + [zeros8, ones24, zeros8]
                                  + list(_split3(ea)) + [jnp.zeros((q - 88, q), F32)], axis=0)
        split_n = split_t.T.astype(BF16)
        v["seg"] = jnp.dot(jnp.concatenate([eye, split_n], axis=1), rhs_ref[d],
                           preferred_element_type=F32)
        v["ea_wide"] = jnp.dot(split_n, expand, preferred_element_type=F32)
        v["cb"] = lax.dot_general(cc, bc, (((1,), (1,)), ((), ())), preferred_element_type=F32)
        b_t = bc.astype(F32).T
        v["b_exit"] = jnp.concatenate([b_t * w_exit[e:e + 1, :] for e in range(e_heads)],
                                      axis=1).astype(BF16)

    def prepare_store(v, slot):
        d = v["d"]
        decay_ref[slot, d] = (jnp.tile(v["cb"], (1, e_heads)) * jnp.exp2(v["seg"])).astype(BF16)
        exit_ref[slot, d] = v["b_exit"]
        entry_ref[slot, d] = v["ea_wide"]

    def emit_entry(d, step):
        rows = chunk_rows(d, step)
        state = state_ref[d]
        w = dict(d=d, rows=rows, state=state)
        if with_output:
            w["y_off"] = jnp.dot(c_ref[rows, :], state.astype(BF16), preferred_element_type=F32)
        return w

    def emit_outputs(w, slot, second_visit):
        d, rows = w["d"], w["rows"]
        last = 0 if d else q - 1
        xc = x_ref[rows, :]
        ssq = jnp.zeros((q, LANES), F32)
        for p in range(e_heads // 2):
            cols = slice(p * LANES, (p + 1) * LANES)
            two = slice(2 * p * q, 2 * (p + 1) * q)
            xp = xc[:, cols]
            x_pair = jnp.concatenate([xp * left_b, xp * right_b], axis=0)
            entry = entry_ref[slot, d, :, cols]
            if not with_output:
                new = jnp.dot(exit_ref[slot, d, :, two], x_pair, preferred_element_type=F32)
                state_ref[d, :, cols] = w["state"][:, cols] * entry[last:last + 1, :] + new
                continue
            lhs = jnp.concatenate([decay_ref[slot, d, :, two], exit_ref[slot, d, :, two]], axis=0)
            out = jnp.dot(lhs, x_pair, preferred_element_type=F32)
            state_ref[d, :, cols] = w["state"][:, cols] * entry[last:last + 1, :] + out[q:]
            y = out[:q] + w["y_off"][:, cols] * entry
            if not second_visit:
                y_ref[rows, cols] = y
                continue
            z = z_ref[rows, cols].astype(F32)
            y = y_ref[rows, cols] + y + xp.astype(F32) * dskip_ref[:, cols]
            gated = y * (z * jax.nn.sigmoid(z))
            u_ref[rows, cols] = gated.astype(u_ref.dtype)
            ssq = ssq + gated * gated
        if with_output and second_visit:
            ssq_ref[rows, :] = ssq

    def prepare_all(step, slot):
        chains = [prepare_cumsum(d, step) for d in range(2)]
        for v in chains:
            prepare_matmuls(v)
        for v in chains:
            prepare_store(v, slot)

    def body(second_visits, final=False):
        def run(it, carry):
            for half in range(2):
                step = 2 * it + half
                look_ahead = not (final and half)
                chains = [prepare_cumsum(d, step + 1) for d in range(2)] if look_ahead else []
                entries = [emit_entry(d, step) for d in range(2)]
                for n, w in enumerate(entries):
                    emit_outputs(w, half, second_visits[half])
                    if look_ahead:
                        prepare_matmuls(chains[n])
                for v in chains:
                    prepare_store(v, 1 - half)
            return carry
        return run

    prepare_all(0, 0)
    n_it = nc // 2
    if nc == 2:
        body((False, True), final=True)(0, 0)
    else:
        lax.fori_loop(0, n_it // 2, body((False, False)), 0)
        lax.fori_loop(n_it // 2, n_it - 1, body((True, True)), 0)
        body((True, True), final=True)(n_it - 1, 0)
    for d in range(2):
        fin_ref[d] = state_ref[d]


def ssd_scan(xbc, dt_t, bias_t, alog_t, init, gate=None):
    b, t, _ = xbc.shape
    nc = t // SSD_CHUNK
    assert nc == 2 or nc % 4 == 0
    wide = HEADS_PER_GROUP * SSD_CHUNK
    g = SSD_GROUPS
    e2 = 2 * HEADS_PER_GROUP
    bm0 = D_INNER // D_STATE
    cm0 = bm0 + g
    state_shape = (2, D_STATE, GROUP_WIDTH)
    group_block = pl.BlockSpec((None, t, GROUP_WIDTH), lambda i, j: (i, 0, j))
    state_block = pl.BlockSpec((None, None) + state_shape, lambda i, j: (i, j, 0, 0, 0))
    in_specs = [group_block,
                pl.BlockSpec((None, t, D_STATE), lambda i, j: (i, 0, bm0 + j)),
                pl.BlockSpec((None, t, D_STATE), lambda i, j: (i, 0, cm0 + j)),
                pl.BlockSpec((e2, t), lambda i, j: (j, i)),
                pl.BlockSpec((None, e2, 1), lambda i, j: (j, 0, 0)),
                pl.BlockSpec((None, e2, 1), lambda i, j: (j, 0, 0))]
    args = [xbc, xbc, xbc, dt_t, bias_t, alog_t]
    out_specs, out_shape = [state_block], [jax.ShapeDtypeStruct((b, g) + state_shape, F32)]
    scratch = [pltpu.VMEM(state_shape, F32),
               pltpu.VMEM((2, 2 * SSD_CHUNK, wide), BF16),
               pltpu.VMEM((2, 2, SSD_CHUNK, wide), BF16),
               pltpu.VMEM((2, 2, D_STATE, wide), BF16),
               pltpu.VMEM((2, 2, SSD_CHUNK, GROUP_WIDTH), F32)]
    if gate is not None:
        p, dskip = gate
        in_specs += [group_block,
                     pl.BlockSpec((None, 1, GROUP_WIDTH), lambda i, j: (j, 0, 0))]
        args += [p, dskip]
        out_specs = [group_block, pl.BlockSpec((None, t, LANES), lambda i, j: (i, 0, j))] + out_specs
        out_shape = [jax.ShapeDtypeStruct((b, t, D_INNER), BF16),
                     jax.ShapeDtypeStruct((b, t, g * LANES), F32)] + out_shape
        scratch.append(pltpu.VMEM((t, GROUP_WIDTH), F32))
    if init is not None:
        in_specs.append(state_block)
        args.append(init)
    outs = pl.pallas_call(
        functools.partial(_ssd_kernel, nc=nc, zero_init=init is None, with_output=gate is not None),
        grid=(b, g),
        in_specs=in_specs,
        out_specs=out_specs,
        out_shape=out_shape,
        scratch_shapes=scratch,
        compiler_params=_params("parallel", "arbitrary"),
        name="ssd_scan",
    )(*args)
    return tuple(outs) if gate is not None else (None, None, outs[0])


def _rope_tables(seq):
    t = jnp.arange(seq)
    row = (t // GRID_W).astype(F32)
    col = (t % GRID_W).astype(F32)
    axis_dim = HEAD_DIM // 2
    inv_freq = ROPE_THETA ** (-jnp.arange(0, axis_dim, 2, dtype=F32) / axis_dim)
    ang_r = row[:, None] * inv_freq[None]
    ang_c = col[:, None] * inv_freq[None]
    cos_r, sin_r, cos_c, sin_c = jnp.cos(ang_r), jnp.sin(ang_r), jnp.cos(ang_c), jnp.sin(ang_c)
    zero = jnp.zeros_like(sin_r)
    cos = jnp.concatenate([cos_r, cos_r, cos_c, cos_c], axis=-1)
    sin_next = jnp.concatenate([-sin_r, zero, -sin_c, zero], axis=-1)
    sin_prev = jnp.concatenate([zero, sin_r, zero, sin_c], axis=-1)
    return cos, sin_next, sin_prev


def _group_major(v):
    return v.reshape(2, SSD_GROUPS, HEADS_PER_GROUP).transpose(1, 0, 2).reshape(
        SSD_GROUPS, 2 * HEADS_PER_GROUP)


def _project(h2, w, shape, name):
    n_out = _weight_array(w).shape[-1]
    return matmul([(h2, w)], [], lambda parts, ex: parts[0], n_out, BF16, tm=2048, tn=1024,
                  name=name).reshape(shape + (n_out,))


def _mixer_side(h, dt_t, w_kvx, w_q, w_zg, conv_w, conv_b, ssd_params, init, dskip):
    b, t, d = h.shape
    h2 = h.reshape(b * t, d)
    kvx = _project(h2, w_kvx, (b, t), "in_proj_kvx")
    q = zg = gate = None
    if dskip is not None:
        q = _project(h2, w_q, (b, t), "in_proj_q")
        zg = _project(h2, w_zg, (b, t), "in_proj_zg")
        gate = (zg, dskip)
    xbc = conv_silu(kvx, conv_w, conv_b)
    u, ssq, fin = ssd_scan(xbc, dt_t, *ssd_params, init, gate)
    return (kvx, q, zg), u, ssq, fin


def _merge_out(att, u, ssq, zg, w_o_attn, w_o_ssd_g, w_out, x, mod, mod_row):
    b, t, d = x.shape
    m = b * t
    tm, tn = min(512, t), 1024
    p2 = zg.reshape(m, zg.shape[-1])
    ga0, gs0 = ZG_GATE_A // tn, ZG_GATE_S // tn
    n_part = ssq.shape[-1]

    def merge(parts, ex):
        ga, gs, sq = ex
        inv_rms = lax.rsqrt(jnp.sum(sq, axis=-1, keepdims=True) * (1.0 / D_INNER) + EPS)
        return (jax.nn.sigmoid(ga.astype(F32)) * parts[0]
                + jax.nn.sigmoid(gs.astype(F32)) * (inv_rms * parts[1]))

    merged = matmul([(att.reshape(m, ATT_WIDTH), w_o_attn), (u.reshape(m, D_INNER), w_o_ssd_g)],
                    [(p2, (tm, tn), lambda i, j: (i, ga0 + j)),
                     (p2, (tm, tn), lambda i, j: (i, gs0 + j)),
                     (ssq.reshape(m, n_part), (tm, n_part), lambda i, j: (i, 0))],
                    merge, d, BF16, tm=tm, tn=tn, w_stationary=True, name="merge")
    return _residual_matmul(merged, w_out, x, mod, mod_row, 2, name="out_proj")


def _residual_matmul(a, w, x, mod, mod_row, gate_idx, name):
    b, t, d = x.shape
    tm = min(512, t)
    tn = d if a.shape[1] <= 4096 else 512
    per_seq = t // tm
    nj = d // tn
    extras = [(x.reshape(b * t, d), (tm, tn), lambda i, j: (i, j)),
              (mod, (None, 1, tn), lambda i, j: (mod_row(i // per_seq), 0, gate_idx * nj + j))]
    res = lambda parts, ex: ex[0] + ex[1] * parts[0]
    out = matmul([(a, w)], extras, res, d, F32, tm=tm, tn=tn, name=name)
    return out.reshape(b, t, d)


def _mlp(x, g, w1, w2, mod, mod_row):
    b, t, d = x.shape
    h = norm_modulate(x, g, mod, mod_row, 3).reshape(b * t, d)
    hid = matmul([(h, w1)], [], lambda parts, ex: jnp.square(jnp.maximum(parts[0], 0.0)), D_FF,
                 BF16, tm=2048, tn=1024, name="ff1")
    return _residual_matmul(hid, w2, x, mod, mod_row, 5, name="ff2")


def kernel(x, c, ctx, c_ctx, w_ada, b_ada, g_norm1, g_norm2, w_in, attn_sink, conv_w, conv_b,
           dt_bias, a_log, d_skip, g_ssd, w_o_attn, w_o_ssd, w_out, w_ff1, w_ff2, g_final):
    depth = w_in.shape[0]
    batch, seq, d = x.shape
    q_scale = LOG2E * HEAD_DIM ** -0.5
    k_tables = _rope_tables(seq)
    q_tables = tuple(tb * q_scale for tb in k_tables)

    n_rows = -(-(batch + 1) // 8) * 8
    cond = jnp.zeros((n_rows, d), F32).at[:batch].set(c).at[batch].set(c_ctx)
    mod = adaln_modulation(cond, w_ada, b_ada[:, None, :]).reshape(depth * n_rows, 1, N_MOD * d)
    w_kvx_all = w_in[:, :, :REF_COL_DT].astype(BF16)
    w_q_all = w_in[:, :, REF_COL_Q:REF_COL_Z].astype(BF16)
    w_zg_all = w_in[:, :, REF_COL_Z:].astype(BF16)
    wa_all, wo_all = w_o_attn.astype(BF16), w_out.astype(BF16)
    ws_all = (g_ssd[:, :, None] * w_o_ssd).astype(BF16)
    w1_all, w2_all = w_ff1.astype(BF16), w_ff2.astype(BF16)

    h_ctx = ctx
    for i in range(depth):
        ctx_out = i < depth - 1
        latent_row = lambda b, i=i: i * n_rows + b
        ctx_row = lambda b, i=i: i * n_rows + batch
        w_kvx, w_q, w_zg = (w_kvx_all, i), (w_q_all, i), (w_zg_all, i)
        wa, ws, wo, w1, w2 = (wa_all, i), (ws_all, i), (wo_all, i), (w1_all, i), (w2_all, i)
        w_dt_t = w_in[i][:, REF_COL_DT:REF_COL_Q].reshape(
            d, 2, SSD_GROUPS, HEADS_PER_GROUP).transpose(2, 1, 3, 0).reshape(
                2 * SSD_HEADS, d).astype(BF16)
        ssd_params = (_group_major(dt_bias[i])[:, :, None], _group_major(a_log[i])[:, :, None])
        dskip = jnp.repeat(d_skip[i][0] + d_skip[i][1], SSD_HEAD_DIM).reshape(
            SSD_GROUPS, 1, GROUP_WIDTH)
        sink = attn_sink[i]
        g1, g2 = g_norm1[i][None], g_norm2[i][None]

        hc, dt_c = norm_modulate(h_ctx, g1, mod, ctx_row, 0, w_dt_t)
        (kvx_c, q_c, zg_c), u_c, ssq_c, fin_c = _mixer_side(
            hc, dt_c, w_kvx, w_q if ctx_out else None, w_zg if ctx_out else None, conv_w[i],
            conv_b[i][None], ssd_params, None, dskip if ctx_out else None)
        hx, dt_x = norm_modulate(x, g1, mod, latent_row, 0, w_dt_t)
        (kvx_x, q_x, zg_x), u_x, ssq_x, _ = _mixer_side(
            hx, dt_x, w_kvx, w_q, w_zg, conv_w[i], conv_b[i][None], ssd_params, fin_c, dskip)
        att_x = latent_attention(q_x, kvx_x, kvx_c, sink, q_tables, k_tables)
        x = _merge_out(att_x, u_x, ssq_x, zg_x, wa, ws, wo, x, mod, latent_row)
        x = _mlp(x, g2, w1, w2, mod, latent_row)
        if ctx_out:
            att_c = context_attention(q_c, kvx_c, sink)
            h_ctx = _merge_out(att_c, u_c, ssq_c, zg_c, wa, ws, wo, h_ctx, mod, ctx_row)
            h_ctx = _mlp(h_ctx, g2, w1, w2, mod, ctx_row)
    return final_norm(x, g_final[None])
```
